```python
import jax, jax.numpy as jnp
from jax import lax
import numpy as np

D_MODEL = 2048
BATCH = 4
SEQ = 8192
DEPTH = 2

CHUNK = 64
N_META = 16
N_MIXERS = 2
D_FF = 5632
POOL_WINDOWS = (2, 4, 8, 16)
N_POOL_GROUPS = len(POOL_WINDOWS)
POOL_GROUP = D_MODEL // N_POOL_GROUPS
SB_HEAD_DIM = 128
SB_HEADS = D_MODEL // SB_HEAD_DIM
Q_BLOCK = 128
N_POOL_LAYERS = (DEPTH + 1) // 2
N_SB_LAYERS = DEPTH // 2
RMS_EPS = 1e-6

kernel_name = 'interleaved_pool_stickbreak_macaron'


def rms_norm(x, g):
    xf = x.astype(jnp.float32)
    y = xf * lax.rsqrt(jnp.mean(xf * xf, axis=-1, keepdims=True) + RMS_EPS)
    return (y * g.astype(jnp.float32)).astype(x.dtype)


def half_swiglu(x, g, w_in, w_out):
    h = rms_norm(x, g)
    gate, up = jnp.split(h @ w_in, 2, axis=-1)
    return x + 0.5 * ((jax.nn.silu(gate) * up) @ w_out)


def pool_mixer(x, g, w_pool, scale):
    T = x.shape[1]
    hf = rms_norm(x, g).astype(jnp.float32)
    count_pos = jnp.arange(T) + 1
    outs = []
    for i, w in enumerate(POOL_WINDOWS):
        hg = hf[..., i * POOL_GROUP:(i + 1) * POOL_GROUP]
        cs = jnp.cumsum(hg, axis=1)
        lag = jnp.pad(cs, ((0, 0), (w, 0), (0, 0)))[:, :T]
        cnt = jnp.minimum(count_pos, w).astype(jnp.float32)[None, :, None]
        y = (cs - lag) / cnt - hg
        outs.append(jnp.einsum('btc,cd->btd', y.astype(x.dtype), w_pool[i]))
    return x + scale * jnp.concatenate(outs, axis=-1)


def stick_breaking_mixer(x, g, w_qkv, qk_norm, w_o):
    B, T, _ = x.shape
    h = rms_norm(x, g)
    q, k, v = jnp.split(h @ w_qkv, 3, axis=-1)
    q = rms_norm(q.reshape(B, T, SB_HEADS, SB_HEAD_DIM), qk_norm[0])
    k = rms_norm(k.reshape(B, T, SB_HEADS, SB_HEAD_DIM), qk_norm[1])
    v = v.reshape(B, T, SB_HEADS, SB_HEAD_DIM)
    n_blocks = -(-T // Q_BLOCK)
    Tp = n_blocks * Q_BLOCK
    pad = ((0, 0), (0, Tp - T), (0, 0), (0, 0))
    q, k, v = [jnp.pad(a, pad).transpose(0, 2, 1, 3) for a in (q, k, v)]
    qf = q.astype(jnp.float32) * (SB_HEAD_DIM ** -0.5)
    kf = k.astype(jnp.float32)
    key_pos = jnp.arange(Tp)

    def block(i):
        start = i * Q_BLOCK
        qb = lax.dynamic_slice_in_dim(qf, start, Q_BLOCK, axis=2)
        z = jnp.einsum('bhqd,bhkd->bhqk', qb, kf)
        q_pos = start + jnp.arange(Q_BLOCK)
        mask = key_pos[None, :] < q_pos[:, None]
        log_beta = jax.nn.log_sigmoid(z)
        log_stay = jnp.where(mask, log_beta - z, 0.0)
        after = lax.cumsum(log_stay, axis=3, reverse=True) - log_stay
        a = jnp.where(mask, jnp.exp(log_beta + after), 0.0)
        return jnp.einsum('bhqk,bhkd->bhqd', a.astype(v.dtype), v)

    o = lax.map(block, jnp.arange(n_blocks))
    o = o.transpose(1, 0, 3, 2, 4).reshape(B, Tp, D_MODEL)[:, :T]
    return x + o @ w_o


def setup_inputs(seed: int = 0) -> dict:
    key = jax.random.key(seed)
    ks = jax.random.split(key, 12)
    f32 = jnp.float32
    x = jax.random.normal(ks[0], (BATCH, SEQ, D_MODEL), f32)
    meta = jax.random.normal(ks[1], (N_META, D_MODEL), f32)
    ffn_norm = 1.0 + 0.02 * jax.random.normal(ks[2], (DEPTH, 2, D_MODEL), f32)
    ffn_w_in = jax.random.normal(ks[3], (DEPTH, 2, D_MODEL, 2 * D_FF), f32) * D_MODEL ** -0.5
    ffn_w_out = jax.random.normal(ks[4], (DEPTH, 2, D_FF, D_MODEL), f32) * D_FF ** -0.5
    mix_norm = 1.0 + 0.02 * jax.random.normal(ks[5], (DEPTH, D_MODEL), f32)
    pool_w = jax.random.normal(ks[6], (N_POOL_LAYERS, N_POOL_GROUPS, POOL_GROUP, POOL_GROUP), f32) * POOL_GROUP ** -0.5
    pool_scale = 1.0 + 0.02 * jax.random.normal(ks[7], (N_POOL_LAYERS, D_MODEL), f32)
    sb_w_qkv = jax.random.normal(ks[8], (N_SB_LAYERS, D_MODEL, 3 * D_MODEL), f32) * D_MODEL ** -0.5
    sb_qk_norm = 1.0 + 0.02 * jax.random.normal(ks[9], (N_SB_LAYERS, 2, SB_HEAD_DIM), f32)
    sb_w_o = jax.random.normal(ks[10], (N_SB_LAYERS, D_MODEL, D_MODEL), f32) * D_MODEL ** -0.5
    return {'x': x, 'meta': meta, 'ffn_norm': ffn_norm, 'ffn_w_in': ffn_w_in, 'ffn_w_out': ffn_w_out,
            'mix_norm': mix_norm, 'pool_w': pool_w, 'pool_scale': pool_scale, 'sb_w_qkv': sb_w_qkv,
            'sb_qk_norm': sb_qk_norm, 'sb_w_o': sb_w_o}


def reference(x, meta, ffn_norm, ffn_w_in, ffn_w_out, mix_norm, pool_w, pool_scale, sb_w_qkv, sb_qk_norm, sb_w_o):
    B = x.shape[0]
    meta_b = jnp.broadcast_to(meta.astype(x.dtype)[None], (B, N_META, D_MODEL))
    h = jnp.concatenate([meta_b, x], axis=1)
    for layer in range(DEPTH):
        h = half_swiglu(h, ffn_norm[layer, 0], ffn_w_in[layer, 0], ffn_w_out[layer, 0])
        j = layer // N_MIXERS
        if layer % N_MIXERS == 0:
            h = pool_mixer(h, mix_norm[layer], pool_w[j], pool_scale[j])
        else:
            h = stick_breaking_mixer(h, mix_norm[layer], sb_w_qkv[j], sb_qk_norm[j], sb_w_o[j])
        h = half_swiglu(h, ffn_norm[layer, 1], ffn_w_in[layer, 1], ffn_w_out[layer, 1])
    return h[:, N_META:]
```

```python
import functools
import math

import jax
import jax.numpy as jnp
from jax import lax
from jax.experimental import pallas as pl
from jax.experimental.pallas import tpu as pltpu

RMS_EPS = 1e-6
POOL_WINDOWS = (2, 4, 8, 16)
MAX_WINDOW = max(POOL_WINDOWS)
HEAD_DIM = 128
LOG2E = math.log2(math.e)

V7X_LANES = 128
V7X_VMEM_BYTES = 64 * 1024 * 1024

F32 = jnp.float32
BF16 = jnp.bfloat16


def _vmem_limit(estimate_bytes):
    return int(min(estimate_bytes * 1.25 + (4 << 20), V7X_VMEM_BYTES * 0.9))


def _rms_norm_rows(x, gain):
    ms = jnp.mean(x * x, axis=-1, keepdims=True)
    return x * lax.rsqrt(ms + RMS_EPS) * gain


def _ffn_kernel(x_ref, g_ref, wg_ref, wu_ref, wo_ref, o_ref, xn_ref, acc_ref):
    j = pl.program_id(1)

    @pl.when(j == 0)
    def _():
        xn_ref[...] = _rms_norm_rows(x_ref[...], g_ref[...]).astype(BF16)
        acc_ref[...] = jnp.zeros_like(acc_ref)

    xn = xn_ref[...]
    gate = jnp.dot(xn, wg_ref[...], preferred_element_type=F32)
    up = jnp.dot(xn, wu_ref[...], preferred_element_type=F32)
    act = (gate * jax.nn.sigmoid(gate) * up).astype(BF16)
    acc_ref[...] += jnp.dot(act, wo_ref[...], preferred_element_type=F32)

    @pl.when(j == pl.num_programs(1) - 1)
    def _():
        o_ref[...] = x_ref[...] + 0.5 * acc_ref[...]


def _ffn_half(x, gain, w_gate, w_up, w_out, *, tm, tf):
    m, d = x.shape
    d_ff = w_gate.shape[1]
    est = (2 * 2 * tm * d * 4 + tm * d * (2 + 4)
           + 2 * (2 * d * tf + tf * d) * 2)
    return pl.pallas_call(
        _ffn_kernel,
        out_shape=jax.ShapeDtypeStruct((m, d), F32),
        grid=(m // tm, d_ff // tf),
        in_specs=[
            pl.BlockSpec((tm, d), lambda i, j: (i, 0)),
            pl.BlockSpec((1, d), lambda i, j: (0, 0)),
            pl.BlockSpec((d, tf), lambda i, j: (0, j)),
            pl.BlockSpec((d, tf), lambda i, j: (0, j)),
            pl.BlockSpec((tf, d), lambda i, j: (j, 0)),
        ],
        out_specs=pl.BlockSpec((tm, d), lambda i, j: (i, 0)),
        scratch_shapes=[pltpu.VMEM((tm, d), BF16), pltpu.VMEM((tm, d), F32)],
        compiler_params=pltpu.CompilerParams(
            dimension_semantics=("parallel", "arbitrary"),
            vmem_limit_bytes=_vmem_limit(est)),
        name="ffn_half",
    )(x, gain.reshape(1, d), w_gate, w_up, w_out)


def _pool_kernel(x_ref, halo_ref, first_halo_ref, g_ref, w_ref, s_ref, o_ref,
                 ext_ref, *, tm, tiles_per_seq, pos0):
    i = pl.program_id(0)
    t = i % tiles_per_seq
    gain = g_ref[...]
    x = x_ref[...]
    halo = jnp.where(t == 0, first_halo_ref[...], halo_ref[...])
    ext_ref[0:MAX_WINDOW, :] = _rms_norm_rows(halo, gain)
    ext_ref[MAX_WINDOW:, :] = _rms_norm_rows(x, gain)
    pos = pos0 + t * tm + lax.broadcasted_iota(jnp.int32, (tm, 1), 0)
    group = x.shape[1] // len(POOL_WINDOWS)
    for gi, w in enumerate(POOL_WINDOWS):
        cols = slice(gi * group, (gi + 1) * group)
        hn = ext_ref[MAX_WINDOW:, cols]
        tot = hn
        for k in range(1, w):
            tot = tot + ext_ref[MAX_WINDOW - k:MAX_WINDOW - k + tm, cols]
        cnt = jnp.minimum(pos + 1, w).astype(F32)
        y = (tot / cnt - hn).astype(BF16)
        mixed = jnp.dot(y, w_ref[gi], preferred_element_type=F32)
        o_ref[:, cols] = x[:, cols] + s_ref[:, cols] * mixed


def _pool_mixer(x, first_halo, gain, w_pool, scale, *, tm, seq, pos0):
    m, d = x.shape
    tiles_per_seq = seq // tm
    halo_blocks = tm // MAX_WINDOW
    est = 2 * 2 * tm * d * 4 + (tm + MAX_WINDOW) * d * 4 + 2 * w_pool.size * 2
    kern = functools.partial(_pool_kernel, tm=tm, tiles_per_seq=tiles_per_seq, pos0=pos0)
    return pl.pallas_call(
        kern,
        out_shape=jax.ShapeDtypeStruct((m, d), F32),
        grid=(m // tm,),
        in_specs=[
            pl.BlockSpec((tm, d), lambda i: (i, 0)),
            pl.BlockSpec((MAX_WINDOW, d), lambda i: (jnp.maximum(i * halo_blocks - 1, 0), 0)),
            pl.BlockSpec((MAX_WINDOW, d), lambda i: (0, 0)),
            pl.BlockSpec((1, d), lambda i: (0, 0)),
            pl.BlockSpec(w_pool.shape, lambda i: (0, 0, 0)),
            pl.BlockSpec((1, d), lambda i: (0, 0)),
        ],
        out_specs=pl.BlockSpec((tm, d), lambda i: (i, 0)),
        scratch_shapes=[pltpu.VMEM((tm + MAX_WINDOW, d), F32)],
        compiler_params=pltpu.CompilerParams(
            dimension_semantics=("arbitrary",),
            vmem_limit_bytes=_vmem_limit(est)),
        name="pool_mixer",
    )(x, x, first_halo, gain.reshape(1, d), w_pool, scale.reshape(1, d))


def _qkv_kernel(x_ref, g_ref, w_ref, qkn_ref, o_ref, xn_ref, *, tn, d):
    j = pl.program_id(1)
    tiles_per_part = d // tn

    @pl.when(j == 0)
    def _():
        xn_ref[...] = _rms_norm_rows(x_ref[...], g_ref[...]).astype(BF16)

    res = jnp.dot(xn_ref[...], w_ref[...], preferred_element_type=F32)

    def head_normed(gain_row, post_scale):
        for hh in range(tn // HEAD_DIM):
            cols = slice(hh * HEAD_DIM, (hh + 1) * HEAD_DIM)
            blk = _rms_norm_rows(res[:, cols], qkn_ref[gain_row:gain_row + 1, :])
            o_ref[:, cols] = (blk * post_scale).astype(o_ref.dtype)

    @pl.when(j < tiles_per_part)
    def _():
        head_normed(0, HEAD_DIM ** -0.5 * LOG2E)

    @pl.when(jnp.logical_and(j >= tiles_per_part, j < 2 * tiles_per_part))
    def _():
        head_normed(1, 1.0)

    @pl.when(j >= 2 * tiles_per_part)
    def _():
        o_ref[...] = res.astype(o_ref.dtype)


def _qkv_proj(x, gain, w_qkv, qk_norm, *, tm, tn):
    m, d = x.shape
    n = w_qkv.shape[1]
    est = 2 * tm * d * 4 + tm * d * 2 + 2 * d * tn * 2 + 2 * tm * tn * 2 + tm * tn * 4
    kern = functools.partial(_qkv_kernel, tn=tn, d=d)
    return pl.pallas_call(
        kern,
        out_shape=jax.ShapeDtypeStruct((m, n), BF16),
        grid=(m // tm, n // tn),
        in_specs=[
            pl.BlockSpec((tm, d), lambda i, j: (i, 0)),
            pl.BlockSpec((1, d), lambda i, j: (0, 0)),
            pl.BlockSpec((d, tn), lambda i, j: (0, j)),
            pl.BlockSpec(qk_norm.shape, lambda i, j: (0, 0)),
        ],
        out_specs=pl.BlockSpec((tm, tn), lambda i, j: (i, j)),
        scratch_shapes=[pltpu.VMEM((tm, d), BF16)],
        compiler_params=pltpu.CompilerParams(
            dimension_semantics=("parallel", "arbitrary"),
            vmem_limit_bytes=_vmem_limit(est)),
        name="qkv_proj",
    )(x, gain.reshape(1, d), w_qkv, qk_norm)


def _sb_block(z, k_mask, tri, r_ref, acc_ref, v_blk, tk):
    e = jnp.exp2(-jnp.abs(z))
    neg_log_stay = jnp.maximum(z, 0.0) + jnp.log(1.0 + e) * LOG2E
    if k_mask is not None:
        neg_log_stay = jnp.where(k_mask, neg_log_stay, 0.0)
    hi = neg_log_stay.astype(BF16)
    lo = (neg_log_stay - hi.astype(F32)).astype(BF16)
    sums = (jnp.dot(hi, tri, preferred_element_type=F32)
            + jnp.dot(lo, tri, preferred_element_type=F32))
    r = r_ref[...]
    after = sums[:, :tk] + r
    a = jnp.exp2(after + (z - neg_log_stay))
    if k_mask is not None:
        a = jnp.where(k_mask, a, 0.0)
    acc_ref[...] += jnp.dot(a.astype(BF16), v_blk, preferred_element_type=F32)
    r_ref[...] = r + sums[:, tk:]


def _sb_kernel(q_ref, k_ref, v_ref, km_ref, vm_ref, tri_ref, o_ref, r_ref, acc_ref,
               *, tq, tk, n_meta):
    qi = pl.program_id(2)
    q = q_ref[...]
    tri = tri_ref[...]
    nt = (((1,), (1,)), ((), ()))
    r_ref[...] = jnp.zeros_like(r_ref)
    acc_ref[...] = jnp.zeros_like(acc_ref)

    row = lax.broadcasted_iota(jnp.int32, (tq, tk), 0)
    col = lax.broadcasted_iota(jnp.int32, (tq, tk), 1)
    blocks_per_q = tq // tk
    for dblk in range(blocks_per_q - 1, -1, -1):
        start = pl.multiple_of(qi * tq + dblk * tk, tk)
        k_blk = k_ref[pl.ds(start, tk), :]
        v_blk = v_ref[pl.ds(start, tk), :]
        z = lax.dot_general(q, k_blk, nt, preferred_element_type=F32)
        _sb_block(z, dblk * tk + col < row, tri, r_ref, acc_ref, v_blk, tk)

    def body(it, carry):
        kb = qi * blocks_per_q - 1 - it
        start = pl.multiple_of(kb * tk, tk)
        k_blk = k_ref[pl.ds(start, tk), :]
        v_blk = v_ref[pl.ds(start, tk), :]
        z = lax.dot_general(q, k_blk, nt, preferred_element_type=F32)
        _sb_block(z, None, tri, r_ref, acc_ref, v_blk, tk)
        return carry

    lax.fori_loop(0, qi * blocks_per_q, body, 0)

    z = lax.dot_general(q, km_ref[...], nt, preferred_element_type=F32)
    _sb_block(z, col < n_meta, tri, r_ref, acc_ref, vm_ref[...], tk)
    o_ref[...] = acc_ref[...].astype(o_ref.dtype)


def _sb_attention(qkv, meta_kv, *, n_batch, seq, n_heads, n_meta, tq, tk):
    m = qkv.shape[0]
    d = n_heads * HEAD_DIM
    nq = seq // tq
    tri_np = jnp.where(
        (lax.broadcasted_iota(jnp.int32, (tk, tk + HEAD_DIM), 0)
         > lax.broadcasted_iota(jnp.int32, (tk, tk + HEAD_DIM), 1))
        | (lax.broadcasted_iota(jnp.int32, (tk, tk + HEAD_DIM), 1) >= tk),
        -1.0, 0.0).astype(BF16)
    est = (2 * tq * HEAD_DIM * 2 * 2 + 2 * 2 * seq * HEAD_DIM * 2
           + 2 * 2 * tk * HEAD_DIM * 2 + 2 * tq * HEAD_DIM * 4
           + 16 * tq * tk * 4)
    kern = functools.partial(_sb_kernel, tq=tq, tk=tk, n_meta=n_meta)
    return pl.pallas_call(
        kern,
        out_shape=jax.ShapeDtypeStruct((m, d), BF16),
        grid=(n_batch, n_heads, nq),
        in_specs=[
            pl.BlockSpec((tq, HEAD_DIM), lambda b, h, i: (b * nq + i, h)),
            pl.BlockSpec((seq, HEAD_DIM), lambda b, h, i: (b, n_heads + h)),
            pl.BlockSpec((seq, HEAD_DIM), lambda b, h, i: (b, 2 * n_heads + h)),
            pl.BlockSpec((tk, HEAD_DIM), lambda b, h, i: (0, n_heads + h)),
            pl.BlockSpec((tk, HEAD_DIM), lambda b, h, i: (0, 2 * n_heads + h)),
            pl.BlockSpec((tk, tk + HEAD_DIM), lambda b, h, i: (0, 0)),
        ],
        out_specs=pl.BlockSpec((tq, HEAD_DIM), lambda b, h, i: (b * nq + i, h)),
        scratch_shapes=[pltpu.VMEM((tq, HEAD_DIM), F32), pltpu.VMEM((tq, HEAD_DIM), F32)],
        compiler_params=pltpu.CompilerParams(
            dimension_semantics=("parallel", "parallel", "arbitrary"),
            vmem_limit_bytes=_vmem_limit(est)),
        name="stick_breaking_attention",
    )(qkv, qkv, qkv, meta_kv, meta_kv, tri_np)


def _oproj_kernel(x_ref, o_ref, w_ref, out_ref):
    out_ref[...] = x_ref[...] + jnp.dot(o_ref[...], w_ref[...], preferred_element_type=F32)


def _out_proj(x, o, w_o, *, tm, tn):
    m, d = x.shape
    est = 2 * (tm * tn * 4 * 2 + tm * d * 2 + d * tn * 2)
    return pl.pallas_call(
        _oproj_kernel,
        out_shape=jax.ShapeDtypeStruct((m, d), F32),
        grid=(m // tm, d // tn),
        in_specs=[
            pl.BlockSpec((tm, tn), lambda i, j: (i, j)),
            pl.BlockSpec((tm, d), lambda i, j: (i, 0)),
            pl.BlockSpec((d, tn), lambda i, j: (0, j)),
        ],
        out_specs=pl.BlockSpec((tm, tn), lambda i, j: (i, j)),
        compiler_params=pltpu.CompilerParams(
            dimension_semantics=("parallel", "arbitrary"),
            vmem_limit_bytes=_vmem_limit(est)),
        name="attn_out_proj",
    )(x, o, w_o)


def kernel(x, meta, ffn_norm, ffn_w_in, ffn_w_out, mix_norm, pool_w, pool_scale,
           sb_w_qkv, sb_qk_norm, sb_w_o):
    n_batch, seq, d = x.shape
    n_meta = meta.shape[0]
    d_ff = ffn_w_out.shape[2]
    n_heads = d // HEAD_DIM
    assert n_meta == MAX_WINDOW and seq % 512 == 0 and d_ff % 512 == 0

    w_gate = ffn_w_in[..., :d_ff].astype(BF16)
    w_up = ffn_w_in[..., d_ff:].astype(BF16)
    w_out = ffn_w_out.astype(BF16)
    w_pool = pool_w[0].astype(BF16)
    w_qkv = sb_w_qkv[0].astype(BF16)
    w_o = sb_w_o[0].astype(BF16)

    def ffn(h, layer, half, tm):
        return _ffn_half(h, ffn_norm[layer, half], w_gate[layer, half], w_up[layer, half],
                         w_out[layer, half], tm=tm, tf=512)

    hm = ffn(meta.astype(F32), 0, 0, n_meta)
    hm_pre_pool = hm
    hm = _pool_mixer(hm, jnp.zeros_like(hm), mix_norm[0], w_pool, pool_scale[0],
                     tm=n_meta, seq=n_meta, pos0=0)
    hm = ffn(hm, 0, 1, n_meta)
    hm = ffn(hm, 1, 0, n_meta)
    meta_qkv = _qkv_proj(hm, mix_norm[1], w_qkv, sb_qk_norm[0], tm=n_meta, tn=512)

    tk = 128
    meta_kv = jnp.pad(meta_qkv, ((0, tk - n_meta), (0, 0)))

    h = x.reshape(n_batch * seq, d)
    h = ffn(h, 0, 0, 512)
    h = _pool_mixer(h, hm_pre_pool, mix_norm[0], w_pool, pool_scale[0],
                    tm=512, seq=seq, pos0=n_meta)
    h = ffn(h, 0, 1, 512)
    h = ffn(h, 1, 0, 512)
    qkv = _qkv_proj(h, mix_norm[1], w_qkv, sb_qk_norm[0], tm=512, tn=512)
    o = _sb_attention(qkv, meta_kv, n_batch=n_batch, seq=seq, n_heads=n_heads,
                      n_meta=n_meta, tq=256, tk=tk)
    h = _out_proj(h, o, w_o, tm=512, tn=1024)
    h = ffn(h, 1, 1, 512)
    return h.reshape(n_batch, seq, d)
```

```python
import functools
import math

import jax
import jax.numpy as jnp
from jax import lax
from jax.experimental import pallas as pl
from jax.experimental.pallas import tpu as pltpu

RMS_EPS = 1e-6
POOL_WINDOWS = (2, 4, 8, 16)
MAX_WINDOW = max(POOL_WINDOWS)
HEAD_DIM = 128
LOG2E = math.log2(math.e)

V7X_LANES = 128
V7X_VMEM_BYTES = 64 * 1024 * 1024

F32 = jnp.float32
BF16 = jnp.bfloat16


def _vmem_limit(estimate_bytes):
    return int(min(estimate_bytes * 1.25 + (4 << 20), V7X_VMEM_BYTES * 0.9))


def _rms_norm_rows(x, gain):
    ms = jnp.mean(x * x, axis=-1, keepdims=True)
    return x * lax.rsqrt(ms + RMS_EPS) * gain


def _ffn_kernel(x_ref, g_ref, wg_ref, wu_ref, wo_ref, o_ref, xn_ref, acc_ref):
    j = pl.program_id(1)

    @pl.when(j == 0)
    def _():
        xn_ref[...] = _rms_norm_rows(x_ref[...], g_ref[...]).astype(BF16)
        acc_ref[...] = jnp.zeros_like(acc_ref)

    xn = xn_ref[...]
    gate = jnp.dot(xn, wg_ref[...], preferred_element_type=F32)
    up = jnp.dot(xn, wu_ref[...], preferred_element_type=F32)
    act = (gate * jax.nn.sigmoid(gate) * up).astype(BF16)
    acc_ref[...] += jnp.dot(act, wo_ref[...], preferred_element_type=F32)

    @pl.when(j == pl.num_programs(1) - 1)
    def _():
        o_ref[...] = x_ref[...] + 0.5 * acc_ref[...]


def _ffn_half(x, gain, w_gate, w_up, w_out, *, tm, tf):
    m, d = x.shape
    d_ff = w_gate.shape[1]
    est = (2 * 2 * tm * d * 4 + tm * d * (2 + 4)
           + 2 * (2 * d * tf + tf * d) * 2)
    return pl.pallas_call(
        _ffn_kernel,
        out_shape=jax.ShapeDtypeStruct((m, d), F32),
        grid=(m // tm, d_ff // tf),
        in_specs=[
            pl.BlockSpec((tm, d), lambda i, j: (i, 0)),
            pl.BlockSpec((1, d), lambda i, j: (0, 0)),
            pl.BlockSpec((d, tf), lambda i, j: (0, j)),
            pl.BlockSpec((d, tf), lambda i, j: (0, j)),
            pl.BlockSpec((tf, d), lambda i, j: (j, 0)),
        ],
        out_specs=pl.BlockSpec((tm, d), lambda i, j: (i, 0)),
        scratch_shapes=[pltpu.VMEM((tm, d), BF16), pltpu.VMEM((tm, d), F32)],
        compiler_params=pltpu.CompilerParams(
            dimension_semantics=("parallel", "arbitrary"),
            vmem_limit_bytes=_vmem_limit(est)),
        name="ffn_half",
    )(x, gain.reshape(1, d), w_gate, w_up, w_out)


def _pool_kernel(x_ref, halo_ref, first_halo_ref, g_ref, w_ref, s_ref, o_ref,
                 ext_ref, *, tm, tiles_per_seq, pos0):
    i = pl.program_id(0)
    t = i % tiles_per_seq
    gain = g_ref[...]
    x = x_ref[...]
    halo = jnp.where(t == 0, first_halo_ref[...], halo_ref[...])
    ext_ref[0:MAX_WINDOW, :] = _rms_norm_rows(halo, gain)
    ext_ref[MAX_WINDOW:, :] = _rms_norm_rows(x, gain)
    pos = pos0 + t * tm + lax.broadcasted_iota(jnp.int32, (tm, 1), 0)
    group = x.shape[1] // len(POOL_WINDOWS)
    for gi, w in enumerate(POOL_WINDOWS):
        cols = slice(gi * group, (gi + 1) * group)
        hn = ext_ref[MAX_WINDOW:, cols]
        tot = hn
        for k in range(1, w):
            tot = tot + ext_ref[MAX_WINDOW - k:MAX_WINDOW - k + tm, cols]
        cnt = jnp.minimum(pos + 1, w).astype(F32)
        y = (tot / cnt - hn).astype(BF16)
        mixed = jnp.dot(y, w_ref[gi], preferred_element_type=F32)
        o_ref[:, cols] = x[:, cols] + s_ref[:, cols] * mixed


def _pool_mixer(x, first_halo, gain, w_pool, scale, *, tm, seq, pos0):
    m, d = x.shape
    tiles_per_seq = seq // tm
    halo_blocks = tm // MAX_WINDOW
    est = 2 * 2 * tm * d * 4 + (tm + MAX_WINDOW) * d * 4 + 2 * w_pool.size * 2
    kern = functools.partial(_pool_kernel, tm=tm, tiles_per_seq=tiles_per_seq, pos0=pos0)
    return pl.pallas_call(
        kern,
        out_shape=jax.ShapeDtypeStruct((m, d), F32),
        grid=(m // tm,),
        in_specs=[
            pl.BlockSpec((tm, d), lambda i: (i, 0)),
            pl.BlockSpec((MAX_WINDOW, d), lambda i: (jnp.maximum(i * halo_blocks - 1, 0), 0)),
            pl.BlockSpec((MAX_WINDOW, d), lambda i: (0, 0)),
            pl.BlockSpec((1, d), lambda i: (0, 0)),
            pl.BlockSpec(w_pool.shape, lambda i: (0, 0, 0)),
            pl.BlockSpec((1, d), lambda i: (0, 0)),
        ],
        out_specs=pl.BlockSpec((tm, d), lambda i: (i, 0)),
        scratch_shapes=[pltpu.VMEM((tm + MAX_WINDOW, d), F32)],
        compiler_params=pltpu.CompilerParams(
            dimension_semantics=("arbitrary",),
            vmem_limit_bytes=_vmem_limit(est)),
        name="pool_mixer",
    )(x, x, first_halo, gain.reshape(1, d), w_pool, scale.reshape(1, d))


def _qkv_kernel(x_ref, g_ref, w_ref, qkn_ref, o_ref, xn_ref, *, tn, d):
    j = pl.program_id(1)
    tiles_per_part = d // tn

    @pl.when(j == 0)
    def _():
        xn_ref[...] = _rms_norm_rows(x_ref[...], g_ref[...]).astype(BF16)

    res = jnp.dot(xn_ref[...], w_ref[...], preferred_element_type=F32)

    def head_normed(gain_row, post_scale):
        for hh in range(tn // HEAD_DIM):
            cols = slice(hh * HEAD_DIM, (hh + 1) * HEAD_DIM)
            blk = _rms_norm_rows(res[:, cols], qkn_ref[gain_row:gain_row + 1, :])
            o_ref[:, cols] = (blk * post_scale).astype(o_ref.dtype)

    @pl.when(j < tiles_per_part)
    def _():
        head_normed(0, HEAD_DIM ** -0.5 * LOG2E)

    @pl.when(jnp.logical_and(j >= tiles_per_part, j < 2 * tiles_per_part))
    def _():
        head_normed(1, 1.0)

    @pl.when(j >= 2 * tiles_per_part)
    def _():
        o_ref[...] = res.astype(o_ref.dtype)


def _qkv_proj(x, gain, w_qkv, qk_norm, *, tm, tn):
    m, d = x.shape
    n = w_qkv.shape[1]
    est = 2 * tm * d * 4 + tm * d * 2 + 2 * d * tn * 2 + 2 * tm * tn * 2 + tm * tn * 4
    kern = functools.partial(_qkv_kernel, tn=tn, d=d)
    return pl.pallas_call(
        kern,
        out_shape=jax.ShapeDtypeStruct((m, n), BF16),
        grid=(m // tm, n // tn),
        in_specs=[
            pl.BlockSpec((tm, d), lambda i, j: (i, 0)),
            pl.BlockSpec((1, d), lambda i, j: (0, 0)),
            pl.BlockSpec((d, tn), lambda i, j: (0, j)),
            pl.BlockSpec(qk_norm.shape, lambda i, j: (0, 0)),
        ],
        out_specs=pl.BlockSpec((tm, tn), lambda i, j: (i, j)),
        scratch_shapes=[pltpu.VMEM((tm, d), BF16)],
        compiler_params=pltpu.CompilerParams(
            dimension_semantics=("parallel", "arbitrary"),
            vmem_limit_bytes=_vmem_limit(est)),
        name="qkv_proj",
    )(x, gain.reshape(1, d), w_qkv, qk_norm)


MASKED_LOGIT = -1e30
PRUNE_LOG2 = -150.0


def _sb_scores(q, k_blk, k_mask, sub):
    tq, n = q.shape[0], k_blk.shape[0]
    z = lax.dot_general(q, k_blk, (((1,), (1,)), ((), ())), preferred_element_type=F32)
    neg_abs = pltpu.bitcast(pltpu.bitcast(z, jnp.uint32) | jnp.uint32(0x80000000), F32)
    neg_log_stay = jnp.maximum(z, 0.0) + jnp.log(1.0 + jnp.exp2(neg_abs)) * LOG2E
    if k_mask is not None:
        neg_log_stay = jnp.where(k_mask, neg_log_stay, 0.0)
        z = jnp.where(k_mask, z, MASKED_LOGIT)
    hi = neg_log_stay.astype(BF16)
    lo = (neg_log_stay - hi.astype(F32)).astype(BF16)
    tots = [jnp.broadcast_to(
        jnp.sum(neg_log_stay[:, j * sub:(j + 1) * sub], axis=1, keepdims=True),
        (tq, V7X_LANES)) for j in range(n // sub)]
    return z, hi, lo, tots


def _sb_accumulate(z, hi, lo, tots, tri, v_blk, r_ref, acc_ref, sub):
    n_sub = len(tots)
    r = r_ref[...]
    parts = [None] * n_sub
    for j in range(n_sub - 1, -1, -1):
        cols = slice(j * sub, (j + 1) * sub)
        log_a = (jnp.dot(hi[:, cols], tri, preferred_element_type=F32)
                 + jnp.dot(lo[:, cols], tri, preferred_element_type=F32)
                 + z[:, cols] + jnp.concatenate([r] * (sub // V7X_LANES), axis=1))
        parts[j] = jnp.exp2(log_a).astype(BF16)
        r = r - tots[j]
    a = parts[0] if n_sub == 1 else jnp.concatenate(parts, axis=1)
    acc_ref[...] += jnp.dot(a, v_blk, preferred_element_type=F32)
    r_ref[...] = r


def _sb_chunk(q, k_blk, v_blk, k_mask, tri, r_ref, acc_ref, sub):
    z, hi, lo, tots = _sb_scores(q, k_blk, k_mask, sub)
    _sb_accumulate(z, hi, lo, tots, tri, v_blk, r_ref, acc_ref, sub)


def _sb_kernel(q_ref, k_ref, v_ref, km_ref, vm_ref, tri_ref, o_ref, r_ref, acc_ref,
               *, tq, n_meta):
    qi = pl.program_id(2)
    q = q_ref[...]
    tri = tri_ref[...]
    r_ref[...] = jnp.zeros_like(r_ref)
    acc_ref[...] = jnp.zeros_like(acc_ref)

    first = jnp.maximum(qi - 1, 0)
    start = pl.multiple_of(first * tq, tq)
    col_minus_row = (lax.broadcasted_iota(jnp.int32, (tq, 2 * tq), 1)
                     - lax.broadcasted_iota(jnp.int32, (tq, 2 * tq), 0))
    causal = col_minus_row < (qi - first) * tq
    _sb_chunk(q, k_ref[pl.ds(start, 2 * tq), :], v_ref[pl.ds(start, 2 * tq), :], causal,
              tri, r_ref, acc_ref, tq)

    def more_keys_matter(state):
        n_done, r_max = state
        return jnp.logical_and(n_done < first, r_max > PRUNE_LOG2)

    def earlier_chunk(state):
        n_done, _ = state
        start = pl.multiple_of((first - 1 - n_done) * tq, tq)
        _sb_chunk(q, k_ref[pl.ds(start, tq), :], v_ref[pl.ds(start, tq), :], None,
                  tri, r_ref, acc_ref, tq)
        return n_done + 1, jnp.max(r_ref[...])

    _, r_max = lax.while_loop(more_keys_matter, earlier_chunk,
                              (jnp.int32(0), jnp.max(r_ref[...])))

    @pl.when(r_max > PRUNE_LOG2)
    def _():
        n_pad = km_ref.shape[0]
        is_meta = lax.broadcasted_iota(jnp.int32, (tq, n_pad), 1) < n_meta
        _sb_chunk(q, km_ref[...], vm_ref[...], is_meta, tri_ref[0:n_pad, 0:n_pad],
                  r_ref, acc_ref, n_pad)

    o_ref[...] = acc_ref[...].astype(o_ref.dtype)


def _sb_attention(qkv, meta_kv, *, n_batch, seq, n_heads, n_meta, tq):
    m = qkv.shape[0]
    d = n_heads * HEAD_DIM
    nq = seq // tq
    n_pad = meta_kv.shape[0]
    assert seq >= 2 * tq and n_pad <= tq
    tri = jnp.where(lax.broadcasted_iota(jnp.int32, (tq, tq), 0)
                    >= lax.broadcasted_iota(jnp.int32, (tq, tq), 1), -1.0, 0.0).astype(BF16)
    est = (2 * 2 * tq * HEAD_DIM * 2 + 2 * 2 * seq * HEAD_DIM * 2
           + 2 * 2 * n_pad * HEAD_DIM * 2 + 2 * tq * tq * 2
           + tq * (V7X_LANES + HEAD_DIM) * 4 + 8 * tq * 2 * tq * 4)
    kern = functools.partial(_sb_kernel, tq=tq, n_meta=n_meta)
    return pl.pallas_call(
        kern,
        out_shape=jax.ShapeDtypeStruct((m, d), BF16),
        grid=(n_batch, n_heads, nq),
        in_specs=[
            pl.BlockSpec((tq, HEAD_DIM), lambda b, h, i: (b * nq + i, h)),
            pl.BlockSpec((seq, HEAD_DIM), lambda b, h, i: (b, n_heads + h)),
            pl.BlockSpec((seq, HEAD_DIM), lambda b, h, i: (b, 2 * n_heads + h)),
            pl.BlockSpec((n_pad, HEAD_DIM), lambda b, h, i: (0, n_heads + h)),
            pl.BlockSpec((n_pad, HEAD_DIM), lambda b, h, i: (0, 2 * n_heads + h)),
            pl.BlockSpec((tq, tq), lambda b, h, i: (0, 0)),
        ],
        out_specs=pl.BlockSpec((tq, HEAD_DIM), lambda b, h, i: (b * nq + i, h)),
        scratch_shapes=[pltpu.VMEM((tq, V7X_LANES), F32), pltpu.VMEM((tq, HEAD_DIM), F32)],
        compiler_params=pltpu.CompilerParams(
            dimension_semantics=("parallel", "parallel", "arbitrary"),
            vmem_limit_bytes=_vmem_limit(est)),
        name="stick_breaking_attention",
    )(qkv, qkv, qkv, meta_kv, meta_kv, tri)


def _oproj_kernel(x_ref, o_ref, w_ref, out_ref):
    out_ref[...] = x_ref[...] + jnp.dot(o_ref[...], w_ref[...], preferred_element_type=F32)


def _out_proj(x, o, w_o, *, tm, tn):
    m, d = x.shape
    est = 2 * (tm * tn * 4 * 2 + tm * d * 2 + d * tn * 2)
    return pl.pallas_call(
        _oproj_kernel,
        out_shape=jax.ShapeDtypeStruct((m, d), F32),
        grid=(m // tm, d // tn),
        in_specs=[
            pl.BlockSpec((tm, tn), lambda i, j: (i, j)),
            pl.BlockSpec((tm, d), lambda i, j: (i, 0)),
            pl.BlockSpec((d, tn), lambda i, j: (0, j)),
        ],
        out_specs=pl.BlockSpec((tm, tn), lambda i, j: (i, j)),
        compiler_params=pltpu.CompilerParams(
            dimension_semantics=("parallel", "arbitrary"),
            vmem_limit_bytes=_vmem_limit(est)),
        name="attn_out_proj",
    )(x, o, w_o)


def kernel(x, meta, ffn_norm, ffn_w_in, ffn_w_out, mix_norm, pool_w, pool_scale,
           sb_w_qkv, sb_qk_norm, sb_w_o):
    n_batch, seq, d = x.shape
    n_meta = meta.shape[0]
    d_ff = ffn_w_out.shape[2]
    n_heads = d // HEAD_DIM
    assert n_meta == MAX_WINDOW and seq % 512 == 0 and d_ff % 512 == 0

    w_gate = ffn_w_in[..., :d_ff].astype(BF16)
    w_up = ffn_w_in[..., d_ff:].astype(BF16)
    w_out = ffn_w_out.astype(BF16)
    w_pool = pool_w[0].astype(BF16)
    w_qkv = sb_w_qkv[0].astype(BF16)
    w_o = sb_w_o[0].astype(BF16)

    def ffn(h, layer, half, tm):
        return _ffn_half(h, ffn_norm[layer, half], w_gate[layer, half], w_up[layer, half],
                         w_out[layer, half], tm=tm, tf=512)

    hm = ffn(meta.astype(F32), 0, 0, n_meta)
    hm_pre_pool = hm
    hm = _pool_mixer(hm, jnp.zeros_like(hm), mix_norm[0], w_pool, pool_scale[0],
                     tm=n_meta, seq=n_meta, pos0=0)
    hm = ffn(hm, 0, 1, n_meta)
    hm = ffn(hm, 1, 0, n_meta)
    meta_qkv = _qkv_proj(hm, mix_norm[1], w_qkv, sb_qk_norm[0], tm=n_meta, tn=512)

    meta_kv = jnp.pad(meta_qkv, ((0, V7X_LANES - n_meta), (0, 0)))

    h = x.reshape(n_batch * seq, d)
    h = ffn(h, 0, 0, 512)
    h = _pool_mixer(h, hm_pre_pool, mix_norm[0], w_pool, pool_scale[0],
                    tm=512, seq=seq, pos0=n_meta)
    h = ffn(h, 0, 1, 512)
    h = ffn(h, 1, 0, 512)
    qkv = _qkv_proj(h, mix_norm[1], w_qkv, sb_qk_norm[0], tm=512, tn=512)
    o = _sb_attention(qkv, meta_kv, n_batch=n_batch, seq=seq, n_heads=n_heads,
                      n_meta=n_meta, tq=256)
    h = _out_proj(h, o, w_o, tm=512, tn=1024)
    h = ffn(h, 1, 1, 512)
    return h.reshape(n_batch, seq, d)
```

```python
import functools
import math

import jax
import jax.numpy as jnp
from jax import lax
from jax.experimental import pallas as pl
from jax.experimental.pallas import tpu as pltpu

RMS_EPS = 1e-6
POOL_WINDOWS = (2, 4, 8, 16)
MAX_WINDOW = max(POOL_WINDOWS)
HEAD_DIM = 128
LOG2E = math.log2(math.e)

V7X_LANES = 128
V7X_VMEM_BYTES = 64 * 1024 * 1024

F32 = jnp.float32
BF16 = jnp.bfloat16

FFN_ROW_TILE = 1024


def _vmem_limit(estimate_bytes):
    return int(min(estimate_bytes * 1.25 + (4 << 20), V7X_VMEM_BYTES * 0.9))


def _rms_norm_rows(x, gain):
    ms = jnp.mean(x * x, axis=-1, keepdims=True)
    return x * lax.rsqrt(ms + RMS_EPS) * gain


def _ffn_kernel(x_ref, g_ref, wg_ref, wu_ref, wo_ref, o_ref, xn_ref):
    j = pl.program_id(1)

    @pl.when(j == 0)
    def _():
        xn_ref[...] = _rms_norm_rows(x_ref[...], g_ref[...]).astype(BF16)
        o_ref[...] = jnp.zeros_like(o_ref)

    xn = xn_ref[...]
    gate = jnp.dot(xn, wg_ref[...], preferred_element_type=F32)
    up = jnp.dot(xn, wu_ref[...], preferred_element_type=F32)
    act = (gate * jax.nn.sigmoid(gate) * up).astype(BF16)
    o_ref[...] += jnp.dot(act, wo_ref[...], preferred_element_type=F32)

    @pl.when(j == pl.num_programs(1) - 1)
    def _():
        o_ref[...] = x_ref[...] + 0.5 * o_ref[...]


def _ffn_half(x, gain, w_in, w_out, layer, half, *, tm, tf):
    m, d = x.shape
    d_ff = w_out.shape[2]
    n_ff = d_ff // tf
    est = (2 * 2 * tm * d * 4 + tm * d * 2 + 2 * (2 * d * tf + tf * d) * 2
           + 3 * tm * tf * 4)
    return pl.pallas_call(
        _ffn_kernel,
        out_shape=jax.ShapeDtypeStruct((m, d), F32),
        grid=(m // tm, n_ff),
        in_specs=[
            pl.BlockSpec((tm, d), lambda i, j: (i, 0)),
            pl.BlockSpec((1, d), lambda i, j: (0, 0)),
            pl.BlockSpec((None, None, d, tf), lambda i, j: (layer, half, 0, j)),
            pl.BlockSpec((None, None, d, tf), lambda i, j: (layer, half, 0, n_ff + j)),
            pl.BlockSpec((None, None, tf, d), lambda i, j: (layer, half, j, 0)),
        ],
        out_specs=pl.BlockSpec((tm, d), lambda i, j: (i, 0)),
        scratch_shapes=[pltpu.VMEM((tm, d), BF16)],
        compiler_params=pltpu.CompilerParams(
            dimension_semantics=("parallel", "arbitrary"),
            vmem_limit_bytes=_vmem_limit(est)),
        name="ffn_half",
    )(x, gain.reshape(1, d), w_in, w_in, w_out)


def _pool_kernel(x_ref, halo_ref, first_halo_ref, g_ref, w_ref, s_ref, o_ref,
                 ext_ref, *, tm, tiles_per_seq, pos0):
    i = pl.program_id(0)
    t = i % tiles_per_seq
    gain = g_ref[...]
    x = x_ref[...]
    halo = jnp.where(t == 0, first_halo_ref[...], halo_ref[...])
    ext_ref[0:MAX_WINDOW, :] = _rms_norm_rows(halo, gain)
    ext_ref[MAX_WINDOW:, :] = _rms_norm_rows(x, gain)
    pos = pos0 + t * tm + lax.broadcasted_iota(jnp.int32, (tm, 1), 0)
    group = x.shape[1] // len(POOL_WINDOWS)
    for gi, w in enumerate(POOL_WINDOWS):
        cols = slice(gi * group, (gi + 1) * group)
        hn = ext_ref[MAX_WINDOW:, cols]
        tot = hn
        for k in range(1, w):
            tot = tot + ext_ref[MAX_WINDOW - k:MAX_WINDOW - k + tm, cols]
        cnt = jnp.minimum(pos + 1, w).astype(F32)
        y = (tot / cnt - hn).astype(BF16)
        mixed = jnp.dot(y, w_ref[gi], preferred_element_type=F32)
        o_ref[:, cols] = x[:, cols] + s_ref[:, cols] * mixed


def _pool_mixer(x, first_halo, gain, w_pool, scale, *, tm, seq, pos0):
    m, d = x.shape
    tiles_per_seq = seq // tm
    halo_blocks = tm // MAX_WINDOW
    est = 2 * 2 * tm * d * 4 + (tm + MAX_WINDOW) * d * 4 + 2 * w_pool.size * 2
    kern = functools.partial(_pool_kernel, tm=tm, tiles_per_seq=tiles_per_seq, pos0=pos0)
    return pl.pallas_call(
        kern,
        out_shape=jax.ShapeDtypeStruct((m, d), F32),
        grid=(m // tm,),
        in_specs=[
            pl.BlockSpec((tm, d), lambda i: (i, 0)),
            pl.BlockSpec((MAX_WINDOW, d), lambda i: (jnp.maximum(i * halo_blocks - 1, 0), 0)),
            pl.BlockSpec((MAX_WINDOW, d), lambda i: (0, 0)),
            pl.BlockSpec((1, d), lambda i: (0, 0)),
            pl.BlockSpec(w_pool.shape, lambda i: (0, 0, 0)),
            pl.BlockSpec((1, d), lambda i: (0, 0)),
        ],
        out_specs=pl.BlockSpec((tm, d), lambda i: (i, 0)),
        scratch_shapes=[pltpu.VMEM((tm + MAX_WINDOW, d), F32)],
        compiler_params=pltpu.CompilerParams(
            dimension_semantics=("arbitrary",),
            vmem_limit_bytes=_vmem_limit(est)),
        name="pool_mixer",
    )(x, x, first_halo, gain.reshape(1, d), w_pool, scale.reshape(1, d))


def _qkv_kernel(x_ref, g_ref, w_ref, cg_ref, o_ref, xn_ref, *, tn, d):
    j = pl.program_id(1)

    @pl.when(j == 0)
    def _():
        xn_ref[...] = _rms_norm_rows(x_ref[...], g_ref[...]).astype(BF16)

    res = jnp.dot(xn_ref[...], w_ref[...], preferred_element_type=F32)
    is_v = j >= 2 * (d // tn)
    for hh in range(tn // HEAD_DIM):
        cols = slice(hh * HEAD_DIM, (hh + 1) * HEAD_DIM)
        blk = res[:, cols]
        ms = jnp.mean(blk * blk, axis=-1, keepdims=True)
        inv = jnp.where(is_v, 1.0, lax.rsqrt(ms + RMS_EPS))
        o_ref[:, cols] = (blk * inv * cg_ref[:, cols]).astype(o_ref.dtype)


def _qkv_proj(x, gain, w_qkv, qk_norm, *, tm, tn):
    m, d = x.shape
    n = w_qkv.shape[1]
    n_heads = d // HEAD_DIM
    col_gain = jnp.concatenate([
        jnp.tile(qk_norm[0].astype(F32) * (HEAD_DIM ** -0.5 * LOG2E), n_heads),
        jnp.tile(qk_norm[1].astype(F32), n_heads),
        jnp.ones((d,), F32)]).reshape(1, n)
    est = 2 * tm * d * 4 + tm * d * 2 + 2 * d * tn * 2 + 2 * tm * tn * 2 + tm * tn * 4
    kern = functools.partial(_qkv_kernel, tn=tn, d=d)
    return pl.pallas_call(
        kern,
        out_shape=jax.ShapeDtypeStruct((m, n), BF16),
        grid=(m // tm, n // tn),
        in_specs=[
            pl.BlockSpec((tm, d), lambda i, j: (i, 0)),
            pl.BlockSpec((1, d), lambda i, j: (0, 0)),
            pl.BlockSpec((d, tn), lambda i, j: (0, j)),
            pl.BlockSpec((1, tn), lambda i, j: (0, j)),
        ],
        out_specs=pl.BlockSpec((tm, tn), lambda i, j: (i, j)),
        scratch_shapes=[pltpu.VMEM((tm, d), BF16)],
        compiler_params=pltpu.CompilerParams(
            dimension_semantics=("parallel", "arbitrary"),
            vmem_limit_bytes=_vmem_limit(est)),
        name="qkv_proj",
    )(x, gain.reshape(1, d), w_qkv, col_gain)


MASKED_LOGIT = -1e30
PRUNE_LOG2 = -150.0


def _sb_scores(q, k_blk, k_mask, sub):
    tq, n = q.shape[0], k_blk.shape[0]
    z = lax.dot_general(q, k_blk, (((1,), (1,)), ((), ())), preferred_element_type=F32)
    if k_mask is not None:
        z = jnp.where(k_mask, z, MASKED_LOGIT)
    neg_abs = pltpu.bitcast(pltpu.bitcast(z, jnp.uint32) | jnp.uint32(0x80000000), F32)
    neg_log_stay = jnp.maximum(z, 0.0) + jnp.log(1.0 + jnp.exp2(neg_abs)) * LOG2E
    hi = neg_log_stay.astype(BF16)
    lo = (neg_log_stay - hi.astype(F32)).astype(BF16)
    tots = [jnp.broadcast_to(
        jnp.sum(neg_log_stay[:, j * sub:(j + 1) * sub], axis=1, keepdims=True),
        (tq, V7X_LANES)) for j in range(n // sub)]
    return z, hi, lo, tots


def _sb_accumulate(z, hi, lo, tots, tri, v_blk, r_ref, acc_ref, sub):
    n_sub = len(tots)
    r = r_ref[...]
    parts = [None] * n_sub
    for j in range(n_sub - 1, -1, -1):
        cols = slice(j * sub, (j + 1) * sub)
        log_a = (jnp.dot(hi[:, cols], tri, preferred_element_type=F32)
                 + jnp.dot(lo[:, cols], tri, preferred_element_type=F32)
                 + z[:, cols] + jnp.concatenate([r] * (sub // V7X_LANES), axis=1))
        parts[j] = jnp.exp2(log_a).astype(BF16)
        r = r - tots[j]
    a = parts[0] if n_sub == 1 else jnp.concatenate(parts, axis=1)
    acc_ref[...] += jnp.dot(a, v_blk, preferred_element_type=F32)
    r_ref[...] = r


def _sb_chunk(q, k_blk, v_blk, k_mask, tri, r_ref, acc_ref, sub):
    z, hi, lo, tots = _sb_scores(q, k_blk, k_mask, sub)
    _sb_accumulate(z, hi, lo, tots, tri, v_blk, r_ref, acc_ref, sub)


def _sb_kernel(q_ref, k_ref, v_ref, km_ref, vm_ref, tri_ref, o_ref, r_ref, acc_ref,
               *, tq, n_meta, heads):
    qi = pl.program_id(2)
    tri = tri_ref[...]
    r_ref[...] = jnp.zeros_like(r_ref)
    acc_ref[...] = jnp.zeros_like(acc_ref)
    head_cols = [slice(hh * HEAD_DIM, (hh + 1) * HEAD_DIM) for hh in range(heads)]

    first = jnp.maximum(qi - 1, 0)
    start = pl.multiple_of(first * tq, tq)
    col_minus_row = (lax.broadcasted_iota(jnp.int32, (tq, 2 * tq), 1)
                     - lax.broadcasted_iota(jnp.int32, (tq, 2 * tq), 0))
    causal = col_minus_row < (qi - first) * tq
    for hh, cols in enumerate(head_cols):
        _sb_chunk(q_ref[:, cols], k_ref[pl.ds(start, 2 * tq), cols],
                  v_ref[pl.ds(start, 2 * tq), cols], causal, tri,
                  r_ref.at[hh], acc_ref.at[hh], tq)

    def more_keys_matter(state):
        n_done, r_max = state
        return jnp.logical_and(n_done < first, r_max > PRUNE_LOG2)

    def earlier_chunk(state):
        n_done, _ = state
        start = pl.multiple_of((first - 1 - n_done) * tq, tq)
        for hh, cols in enumerate(head_cols):
            _sb_chunk(q_ref[:, cols], k_ref[pl.ds(start, tq), cols],
                      v_ref[pl.ds(start, tq), cols], None, tri,
                      r_ref.at[hh], acc_ref.at[hh], tq)
        return n_done + 1, jnp.max(r_ref[...])

    _, r_max = lax.while_loop(more_keys_matter, earlier_chunk,
                              (jnp.int32(0), jnp.max(r_ref[...])))

    @pl.when(r_max > PRUNE_LOG2)
    def _():
        n_pad = km_ref.shape[0]
        is_meta = lax.broadcasted_iota(jnp.int32, (tq, n_pad), 1) < n_meta
        for hh, cols in enumerate(head_cols):
            _sb_chunk(q_ref[:, cols], km_ref[:, cols], vm_ref[:, cols], is_meta,
                      tri_ref[0:n_pad, 0:n_pad], r_ref.at[hh], acc_ref.at[hh], n_pad)

    for hh, cols in enumerate(head_cols):
        o_ref[:, cols] = acc_ref[hh].astype(o_ref.dtype)


def _sb_attention(qkv, meta_kv, *, n_batch, seq, n_heads, n_meta, tq, heads):
    m = qkv.shape[0]
    d = n_heads * HEAD_DIM
    nq = seq // tq
    n_pad = meta_kv.shape[0]
    assert seq >= 2 * tq and n_pad <= tq and n_heads % heads == 0
    groups = n_heads // heads
    width = heads * HEAD_DIM
    tri = jnp.where(lax.broadcasted_iota(jnp.int32, (tq, tq), 0)
                    >= lax.broadcasted_iota(jnp.int32, (tq, tq), 1), -1.0, 0.0).astype(BF16)
    est = (2 * 2 * tq * width * 2 + 2 * 2 * seq * width * 2
           + 2 * 2 * n_pad * width * 2 + 2 * tq * tq * 2
           + heads * tq * (V7X_LANES + HEAD_DIM) * 4 + heads * 8 * tq * 2 * tq * 4)
    kern = functools.partial(_sb_kernel, tq=tq, n_meta=n_meta, heads=heads)
    return pl.pallas_call(
        kern,
        out_shape=jax.ShapeDtypeStruct((m, d), BF16),
        grid=(n_batch, groups, nq),
        in_specs=[
            pl.BlockSpec((tq, width), lambda b, g, i: (b * nq + i, g)),
            pl.BlockSpec((seq, width), lambda b, g, i: (b, groups + g)),
            pl.BlockSpec((seq, width), lambda b, g, i: (b, 2 * groups + g)),
            pl.BlockSpec((n_pad, width), lambda b, g, i: (0, groups + g)),
            pl.BlockSpec((n_pad, width), lambda b, g, i: (0, 2 * groups + g)),
            pl.BlockSpec((tq, tq), lambda b, g, i: (0, 0)),
        ],
        out_specs=pl.BlockSpec((tq, width), lambda b, g, i: (b * nq + i, g)),
        scratch_shapes=[pltpu.VMEM((heads, tq, V7X_LANES), F32),
                        pltpu.VMEM((heads, tq, HEAD_DIM), F32)],
        compiler_params=pltpu.CompilerParams(
            dimension_semantics=("parallel", "parallel", "arbitrary"),
            vmem_limit_bytes=_vmem_limit(est)),
        name="stick_breaking_attention",
    )(qkv, qkv, qkv, meta_kv, meta_kv, tri)


def _oproj_kernel(x_ref, o_ref, w_ref, out_ref):
    out_ref[...] = x_ref[...] + jnp.dot(o_ref[...], w_ref[...], preferred_element_type=F32)


def _out_proj(x, o, w_o, *, tm, tn):
    m, d = x.shape
    est = 2 * (tm * tn * 4 * 2 + tm * d * 2 + d * tn * 2)
    return pl.pallas_call(
        _oproj_kernel,
        out_shape=jax.ShapeDtypeStruct((m, d), F32),
        grid=(m // tm, d // tn),
        in_specs=[
            pl.BlockSpec((tm, tn), lambda i, j: (i, j)),
            pl.BlockSpec((tm, d), lambda i, j: (i, 0)),
            pl.BlockSpec((d, tn), lambda i, j: (0, j)),
        ],
        out_specs=pl.BlockSpec((tm, tn), lambda i, j: (i, j)),
        compiler_params=pltpu.CompilerParams(
            dimension_semantics=("parallel", "arbitrary"),
            vmem_limit_bytes=_vmem_limit(est)),
        name="attn_out_proj",
    )(x, o, w_o)


def kernel(x, meta, ffn_norm, ffn_w_in, ffn_w_out, mix_norm, pool_w, pool_scale,
           sb_w_qkv, sb_qk_norm, sb_w_o):
    n_batch, seq, d = x.shape
    n_meta = meta.shape[0]
    d_ff = ffn_w_out.shape[2]
    n_heads = d // HEAD_DIM
    assert n_meta == MAX_WINDOW and seq % 512 == 0 and d_ff % 512 == 0

    w_in = ffn_w_in.astype(BF16)
    w_out = ffn_w_out.astype(BF16)
    w_pool = pool_w[0].astype(BF16)
    w_qkv = sb_w_qkv[0].astype(BF16)
    w_o = sb_w_o[0].astype(BF16)

    def ffn(h, layer, half, tm):
        return _ffn_half(h, ffn_norm[layer, half], w_in, w_out, layer, half, tm=tm, tf=512)

    hm = ffn(meta.astype(F32), 0, 0, n_meta)
    hm_pre_pool = hm
    hm = _pool_mixer(hm, jnp.zeros_like(hm), mix_norm[0], w_pool, pool_scale[0],
                     tm=n_meta, seq=n_meta, pos0=0)
    hm = ffn(hm, 0, 1, n_meta)
    hm = ffn(hm, 1, 0, n_meta)
    meta_qkv = _qkv_proj(hm, mix_norm[1], w_qkv, sb_qk_norm[0], tm=n_meta, tn=512)

    meta_kv = jnp.pad(meta_qkv, ((0, V7X_LANES - n_meta), (0, 0)))

    h = x.reshape(n_batch * seq, d)
    h = ffn(h, 0, 0, FFN_ROW_TILE)
    h = _pool_mixer(h, hm_pre_pool, mix_norm[0], w_pool, pool_scale[0],
                    tm=512, seq=seq, pos0=n_meta)
    h = ffn(h, 0, 1, FFN_ROW_TILE)
    h = ffn(h, 1, 0, FFN_ROW_TILE)
    qkv = _qkv_proj(h, mix_norm[1], w_qkv, sb_qk_norm[0], tm=512, tn=2048)
    o = _sb_attention(qkv, meta_kv, n_batch=n_batch, seq=seq, n_heads=n_heads,
                      n_meta=n_meta, tq=256, heads=4)
    h = _out_proj(h, o, w_o, tm=512, tn=1024)
    h = ffn(h, 1, 1, FFN_ROW_TILE)
    return h.reshape(n_batch, seq, d)
```

```python
import functools
import math

import jax
import jax.numpy as jnp
from jax import lax
from jax.experimental import pallas as pl
from jax.experimental.pallas import tpu as pltpu

RMS_EPS = 1e-6
POOL_WINDOWS = (2, 4, 8, 16)
MAX_WINDOW = max(POOL_WINDOWS)
HEAD_DIM = 128
LOG2E = math.log2(math.e)

V7X_LANES = 128
V7X_VMEM_BYTES = 64 * 1024 * 1024

F32 = jnp.float32
BF16 = jnp.bfloat16

FFN_ROW_TILE = 1024


def _vmem_limit(estimate_bytes):
    return int(min(estimate_bytes * 1.25 + (4 << 20), V7X_VMEM_BYTES * 0.9))


def _rms_norm_rows(x, gain):
    ms = jnp.mean(x * x, axis=-1, keepdims=True)
    return x * lax.rsqrt(ms + RMS_EPS) * gain


def _ffn_kernel(x_ref, g_ref, wg_ref, wu_ref, wo_ref, o_ref, xn_ref):
    j = pl.program_id(1)

    @pl.when(j == 0)
    def _():
        x = x_ref[...]
        xn_ref[...] = _rms_norm_rows(x, g_ref[...]).astype(BF16)
        o_ref[...] = x

    xn = xn_ref[...]
    gate = jnp.dot(xn, wg_ref[...], preferred_element_type=F32)
    up = jnp.dot(xn, wu_ref[...], preferred_element_type=F32)
    act = (gate * jax.nn.sigmoid(gate) * up).astype(BF16)
    o_ref[...] += jnp.dot(act, wo_ref[...], preferred_element_type=F32)


def _ffn_half(x, gain, w_in, w_out, layer, half, *, tm, tf):
    m, d = x.shape
    d_ff = w_out.shape[2]
    n_ff = d_ff // tf
    est = (2 * 2 * tm * d * 4 + tm * d * 2 + 2 * (2 * d * tf + tf * d) * 2
           + 3 * tm * tf * 4)
    return pl.pallas_call(
        _ffn_kernel,
        out_shape=jax.ShapeDtypeStruct((m, d), F32),
        grid=(m // tm, n_ff),
        in_specs=[
            pl.BlockSpec((tm, d), lambda i, j: (i, 0)),
            pl.BlockSpec((1, d), lambda i, j: (0, 0)),
            pl.BlockSpec((None, None, d, tf), lambda i, j: (layer, half, 0, j)),
            pl.BlockSpec((None, None, d, tf), lambda i, j: (layer, half, 0, n_ff + j)),
            pl.BlockSpec((None, None, tf, d), lambda i, j: (layer, half, j, 0)),
        ],
        out_specs=pl.BlockSpec((tm, d), lambda i, j: (i, 0)),
        scratch_shapes=[pltpu.VMEM((tm, d), BF16)],
        compiler_params=pltpu.CompilerParams(
            dimension_semantics=("parallel", "arbitrary"),
            vmem_limit_bytes=_vmem_limit(est)),
        name="ffn_half",
    )(x, gain.reshape(1, d), w_in, w_in, w_out)


def _pool_kernel(x_ref, halo_ref, first_halo_ref, g_ref, w_ref, s_ref, o_ref,
                 ext_ref, *, tm, tiles_per_seq, pos0):
    i = pl.program_id(0)
    t = i % tiles_per_seq
    gain = g_ref[...]
    x = x_ref[...]
    halo = jnp.where(t == 0, first_halo_ref[...], halo_ref[...])
    ext_ref[0:MAX_WINDOW, :] = _rms_norm_rows(halo, gain)
    ext_ref[MAX_WINDOW:, :] = _rms_norm_rows(x, gain)
    pos = pos0 + t * tm + lax.broadcasted_iota(jnp.int32, (tm, 1), 0)
    group = x.shape[1] // len(POOL_WINDOWS)
    for gi, w in enumerate(POOL_WINDOWS):
        assert w & (w - 1) == 0 and w <= MAX_WINDOW
        cols = slice(gi * group, (gi + 1) * group)
        tot = ext_ref[:, cols]
        step = 1
        while step < w:
            tot = tot + pltpu.roll(tot, step, axis=0)
            step *= 2
        tot = tot[MAX_WINDOW:]
        hn = ext_ref[MAX_WINDOW:, cols]
        cnt = jnp.minimum(pos + 1, w).astype(F32)
        y = (tot / cnt - hn).astype(BF16)
        mixed = jnp.dot(y, w_ref[gi], preferred_element_type=F32)
        o_ref[:, cols] = x[:, cols] + s_ref[:, cols] * mixed


def _pool_mixer(x, first_halo, gain, w_pool, scale, *, tm, seq, pos0):
    m, d = x.shape
    tiles_per_seq = seq // tm
    halo_blocks = tm // MAX_WINDOW
    est = 2 * 2 * tm * d * 4 + (tm + MAX_WINDOW) * d * 4 + 2 * w_pool.size * 2
    kern = functools.partial(_pool_kernel, tm=tm, tiles_per_seq=tiles_per_seq, pos0=pos0)
    return pl.pallas_call(
        kern,
        out_shape=jax.ShapeDtypeStruct((m, d), F32),
        grid=(m // tm,),
        in_specs=[
            pl.BlockSpec((tm, d), lambda i: (i, 0)),
            pl.BlockSpec((MAX_WINDOW, d), lambda i: (jnp.maximum(i * halo_blocks - 1, 0), 0)),
            pl.BlockSpec((MAX_WINDOW, d), lambda i: (0, 0)),
            pl.BlockSpec((1, d), lambda i: (0, 0)),
            pl.BlockSpec(w_pool.shape, lambda i: (0, 0, 0)),
            pl.BlockSpec((1, d), lambda i: (0, 0)),
        ],
        out_specs=pl.BlockSpec((tm, d), lambda i: (i, 0)),
        scratch_shapes=[pltpu.VMEM((tm + MAX_WINDOW, d), F32)],
        compiler_params=pltpu.CompilerParams(
            dimension_semantics=("arbitrary",),
            vmem_limit_bytes=_vmem_limit(est)),
        name="pool_mixer",
    )(x, x, first_halo, gain.reshape(1, d), w_pool, scale.reshape(1, d))


def _qkv_kernel(x_ref, g_ref, w_ref, cg_ref, o_ref, xn_ref, *, tn, d):
    j = pl.program_id(1)

    @pl.when(j == 0)
    def _():
        xn_ref[...] = _rms_norm_rows(x_ref[...], g_ref[...]).astype(BF16)

    res = jnp.dot(xn_ref[...], w_ref[...], preferred_element_type=F32)
    is_v = j >= 2 * (d // tn)
    for hh in range(tn // HEAD_DIM):
        cols = slice(hh * HEAD_DIM, (hh + 1) * HEAD_DIM)
        blk = res[:, cols]
        ms = jnp.mean(blk * blk, axis=-1, keepdims=True)
        inv = jnp.where(is_v, 1.0, lax.rsqrt(ms + RMS_EPS))
        o_ref[:, cols] = (blk * inv * cg_ref[:, cols]).astype(o_ref.dtype)


def _qkv_proj(x, gain, w_qkv, qk_norm, *, tm, tn):
    m, d = x.shape
    n = w_qkv.shape[1]
    n_heads = d // HEAD_DIM
    col_gain = jnp.concatenate([
        jnp.tile(qk_norm[0].astype(F32) * (HEAD_DIM ** -0.5 * LOG2E), n_heads),
        jnp.tile(qk_norm[1].astype(F32), n_heads),
        jnp.ones((d,), F32)]).reshape(1, n)
    est = 2 * tm * d * 4 + tm * d * 2 + 2 * d * tn * 2 + 2 * tm * tn * 2 + tm * tn * 4
    kern = functools.partial(_qkv_kernel, tn=tn, d=d)
    return pl.pallas_call(
        kern,
        out_shape=jax.ShapeDtypeStruct((m, n), BF16),
        grid=(m // tm, n // tn),
        in_specs=[
            pl.BlockSpec((tm, d), lambda i, j: (i, 0)),
            pl.BlockSpec((1, d), lambda i, j: (0, 0)),
            pl.BlockSpec((d, tn), lambda i, j: (0, j)),
            pl.BlockSpec((1, tn), lambda i, j: (0, j)),
        ],
        out_specs=pl.BlockSpec((tm, tn), lambda i, j: (i, j)),
        scratch_shapes=[pltpu.VMEM((tm, d), BF16)],
        compiler_params=pltpu.CompilerParams(
            dimension_semantics=("parallel", "arbitrary"),
            vmem_limit_bytes=_vmem_limit(est)),
        name="qkv_proj",
    )(x, gain.reshape(1, d), w_qkv, col_gain)


MASKED_LOGIT = -1e30
PRUNE_LOG2 = -150.0


def _sb_scores(q, k_blk, k_mask, sub):
    tq, n = q.shape[0], k_blk.shape[0]
    z = lax.dot_general(q, k_blk, (((1,), (1,)), ((), ())), preferred_element_type=F32)
    if k_mask is not None:
        z = jnp.where(k_mask, z, MASKED_LOGIT)
    neg_abs = pltpu.bitcast(pltpu.bitcast(z, jnp.uint32) | jnp.uint32(0x80000000), F32)
    neg_log_stay = jnp.maximum(z, 0.0) + jnp.log(1.0 + jnp.exp2(neg_abs)) * LOG2E
    hi = neg_log_stay.astype(BF16)
    lo = (neg_log_stay - hi.astype(F32)).astype(BF16)
    tots = [jnp.broadcast_to(
        jnp.sum(neg_log_stay[:, j * sub:(j + 1) * sub], axis=1, keepdims=True),
        (tq, V7X_LANES)) for j in range(n // sub)]
    return z, hi, lo, tots


def _sb_accumulate(z, hi, lo, tots, tri, v_blk, r_ref, acc_ref, sub):
    n_sub = len(tots)
    r = r_ref[...]
    parts = [None] * n_sub
    for j in range(n_sub - 1, -1, -1):
        cols = slice(j * sub, (j + 1) * sub)
        log_a = (jnp.dot(hi[:, cols], tri, preferred_element_type=F32)
                 + jnp.dot(lo[:, cols], tri, preferred_element_type=F32)
                 + z[:, cols] + jnp.concatenate([r] * (sub // V7X_LANES), axis=1))
        parts[j] = jnp.exp2(log_a).astype(BF16)
        r = r - tots[j]
    a = parts[0] if n_sub == 1 else jnp.concatenate(parts, axis=1)
    acc_ref[...] += jnp.dot(a, v_blk, preferred_element_type=F32)
    r_ref[...] = r


def _sb_chunk(q, k_blk, v_blk, k_mask, tri, r_ref, acc_ref, sub):
    z, hi, lo, tots = _sb_scores(q, k_blk, k_mask, sub)
    _sb_accumulate(z, hi, lo, tots, tri, v_blk, r_ref, acc_ref, sub)


def _sb_kernel(q_ref, k_ref, v_ref, km_ref, vm_ref, tri_ref, o_ref, r_ref, acc_ref,
               *, tq, n_meta, heads):
    qi = pl.program_id(2)
    tri = tri_ref[...]
    r_ref[...] = jnp.zeros_like(r_ref)
    acc_ref[...] = jnp.zeros_like(acc_ref)
    head_cols = [slice(hh * HEAD_DIM, (hh + 1) * HEAD_DIM) for hh in range(heads)]

    first = jnp.maximum(qi - 1, 0)
    start = pl.multiple_of(first * tq, tq)
    col_minus_row = (lax.broadcasted_iota(jnp.int32, (tq, 2 * tq), 1)
                     - lax.broadcasted_iota(jnp.int32, (tq, 2 * tq), 0))
    causal = col_minus_row < (qi - first) * tq
    for hh, cols in enumerate(head_cols):
        _sb_chunk(q_ref[:, cols], k_ref[pl.ds(start, 2 * tq), cols],
                  v_ref[pl.ds(start, 2 * tq), cols], causal, tri,
                  r_ref.at[hh], acc_ref.at[hh], tq)

    def more_keys_matter(state):
        n_done, r_max = state
        return jnp.logical_and(n_done < first, r_max > PRUNE_LOG2)

    def earlier_chunk(state):
        n_done, _ = state
        start = pl.multiple_of((first - 1 - n_done) * tq, tq)
        for hh, cols in enumerate(head_cols):
            _sb_chunk(q_ref[:, cols], k_ref[pl.ds(start, tq), cols],
                      v_ref[pl.ds(start, tq), cols], None, tri,
                      r_ref.at[hh], acc_ref.at[hh], tq)
        return n_done + 1, jnp.max(r_ref[...])

    _, r_max = lax.while_loop(more_keys_matter, earlier_chunk,
                              (jnp.int32(0), jnp.max(r_ref[...])))

    @pl.when(r_max > PRUNE_LOG2)
    def _():
        n_pad = km_ref.shape[0]
        is_meta = lax.broadcasted_iota(jnp.int32, (tq, n_pad), 1) < n_meta
        for hh, cols in enumerate(head_cols):
            _sb_chunk(q_ref[:, cols], km_ref[:, cols], vm_ref[:, cols], is_meta,
                      tri_ref[0:n_pad, 0:n_pad], r_ref.at[hh], acc_ref.at[hh], n_pad)

    for hh, cols in enumerate(head_cols):
        o_ref[:, cols] = acc_ref[hh].astype(o_ref.dtype)


def _sb_attention(qkv, meta_kv, *, n_batch, seq, n_heads, n_meta, tq, heads):
    m = qkv.shape[0]
    d = n_heads * HEAD_DIM
    nq = seq // tq
    n_pad = meta_kv.shape[0]
    assert seq >= 2 * tq and n_pad <= tq and n_heads % heads == 0
    groups = n_heads // heads
    width = heads * HEAD_DIM
    tri = jnp.where(lax.broadcasted_iota(jnp.int32, (tq, tq), 0)
                    >= lax.broadcasted_iota(jnp.int32, (tq, tq), 1), -1.0, 0.0).astype(BF16)
    est = (2 * 2 * tq * width * 2 + 2 * 2 * seq * width * 2
           + 2 * 2 * n_pad * width * 2 + 2 * tq * tq * 2
           + heads * tq * (V7X_LANES + HEAD_DIM) * 4 + heads * 8 * tq * 2 * tq * 4)
    kern = functools.partial(_sb_kernel, tq=tq, n_meta=n_meta, heads=heads)
    return pl.pallas_call(
        kern,
        out_shape=jax.ShapeDtypeStruct((m, d), BF16),
        grid=(n_batch, groups, nq),
        in_specs=[
            pl.BlockSpec((tq, width), lambda b, g, i: (b * nq + i, g)),
            pl.BlockSpec((seq, width), lambda b, g, i: (b, groups + g)),
            pl.BlockSpec((seq, width), lambda b, g, i: (b, 2 * groups + g)),
            pl.BlockSpec((n_pad, width), lambda b, g, i: (0, groups + g)),
            pl.BlockSpec((n_pad, width), lambda b, g, i: (0, 2 * groups + g)),
            pl.BlockSpec((tq, tq), lambda b, g, i: (0, 0)),
        ],
        out_specs=pl.BlockSpec((tq, width), lambda b, g, i: (b * nq + i, g)),
        scratch_shapes=[pltpu.VMEM((heads, tq, V7X_LANES), F32),
                        pltpu.VMEM((heads, tq, HEAD_DIM), F32)],
        compiler_params=pltpu.CompilerParams(
            dimension_semantics=("parallel", "parallel", "arbitrary"),
            vmem_limit_bytes=_vmem_limit(est)),
        name="stick_breaking_attention",
    )(qkv, qkv, qkv, meta_kv, meta_kv, tri)


def _oproj_kernel(x_ref, o_ref, w_ref, out_ref):
    out_ref[...] = x_ref[...] + jnp.dot(o_ref[...], w_ref[...], preferred_element_type=F32)


def _out_proj(x, o, w_o, *, tm, tn):
    m, d = x.shape
    est = 2 * (tm * tn * 4 * 2 + tm * d * 2 + d * tn * 2)
    return pl.pallas_call(
        _oproj_kernel,
        out_shape=jax.ShapeDtypeStruct((m, d), F32),
        grid=(m // tm, d // tn),
        in_specs=[
            pl.BlockSpec((tm, tn), lambda i, j: (i, j)),
            pl.BlockSpec((tm, d), lambda i, j: (i, 0)),
            pl.BlockSpec((d, tn), lambda i, j: (0, j)),
        ],
        out_specs=pl.BlockSpec((tm, tn), lambda i, j: (i, j)),
        compiler_params=pltpu.CompilerParams(
            dimension_semantics=("parallel", "arbitrary"),
            vmem_limit_bytes=_vmem_limit(est)),
        name="attn_out_proj",
    )(x, o, w_o)


def kernel(x, meta, ffn_norm, ffn_w_in, ffn_w_out, mix_norm, pool_w, pool_scale,
           sb_w_qkv, sb_qk_norm, sb_w_o):
    n_batch, seq, d = x.shape
    n_meta = meta.shape[0]
    d_ff = ffn_w_out.shape[2]
    n_heads = d // HEAD_DIM
    assert n_meta == MAX_WINDOW and seq % 512 == 0 and d_ff % 512 == 0

    w_in = ffn_w_in.astype(BF16)
    w_out = (0.5 * ffn_w_out).astype(BF16)
    w_pool = pool_w[0].astype(BF16)
    w_qkv = sb_w_qkv[0].astype(BF16)
    w_o = sb_w_o[0].astype(BF16)

    def ffn(h, layer, half, tm):
        return _ffn_half(h, ffn_norm[layer, half], w_in, w_out, layer, half, tm=tm, tf=512)

    hm = ffn(meta.astype(F32), 0, 0, n_meta)
    hm_pre_pool = hm
    hm = _pool_mixer(hm, jnp.zeros_like(hm), mix_norm[0], w_pool, pool_scale[0],
                     tm=n_meta, seq=n_meta, pos0=0)
    hm = ffn(hm, 0, 1, n_meta)
    hm = ffn(hm, 1, 0, n_meta)
    meta_qkv = _qkv_proj(hm, mix_norm[1], w_qkv, sb_qk_norm[0], tm=n_meta, tn=512)

    meta_kv = jnp.pad(meta_qkv, ((0, V7X_LANES - n_meta), (0, 0)))

    h = x.reshape(n_batch * seq, d)
    h = ffn(h, 0, 0, FFN_ROW_TILE)
    h = _pool_mixer(h, hm_pre_pool, mix_norm[0], w_pool, pool_scale[0],
                    tm=512, seq=seq, pos0=n_meta)
    h = ffn(h, 0, 1, FFN_ROW_TILE)
    h = ffn(h, 1, 0, FFN_ROW_TILE)
    qkv = _qkv_proj(h, mix_norm[1], w_qkv, sb_qk_norm[0], tm=512, tn=2048)
    o = _sb_attention(qkv, meta_kv, n_batch=n_batch, seq=seq, n_heads=n_heads,
                      n_meta=n_meta, tq=256, heads=4)
    h = _out_proj(h, o, w_o, tm=512, tn=2048)
    h = ffn(h, 1, 1, FFN_ROW_TILE)
    return h.reshape(n_batch, seq, d)
```

```python
import functools
import math

import jax
import jax.numpy as jnp
from jax import lax
from jax.experimental import pallas as pl
from jax.experimental.pallas import tpu as pltpu

RMS_EPS = 1e-6
POOL_WINDOWS = (2, 4, 8, 16)
MAX_WINDOW = max(POOL_WINDOWS)
HEAD_DIM = 128
LOG2E = math.log2(math.e)

V7X_LANES = 128
V7X_VMEM_BYTES = 64 * 1024 * 1024

F32 = jnp.float32
BF16 = jnp.bfloat16

FFN_ROW_TILE = 1024


def _vmem_limit(estimate_bytes):
    return int(min(estimate_bytes * 1.25 + (4 << 20), V7X_VMEM_BYTES * 0.9))


def _rms_norm_rows(x, gain):
    ms = jnp.mean(x * x, axis=-1, keepdims=True)
    return x * lax.rsqrt(ms + RMS_EPS) * gain


def _ffn_kernel(*refs, cast_next):
    if cast_next:
        (x_ref, g_ref, wg_ref, wu_ref, wo_ref, next_wi_ref, next_wo_ref,
         o_ref, next_wi_bf_ref, next_wo_bf_ref, xn_ref) = refs
    else:
        x_ref, g_ref, wg_ref, wu_ref, wo_ref, o_ref, xn_ref = refs
    j = pl.program_id(1)

    @pl.when(j == 0)
    def _():
        x = x_ref[...]
        xn_ref[...] = _rms_norm_rows(x, g_ref[...]).astype(BF16)
        o_ref[...] = x

    xn = xn_ref[...]
    gate = jnp.dot(xn, wg_ref[...], preferred_element_type=F32)
    up = jnp.dot(xn, wu_ref[...], preferred_element_type=F32)
    act = (gate * jax.nn.sigmoid(gate) * up).astype(BF16)
    o_ref[...] += jnp.dot(act, wo_ref[...], preferred_element_type=F32)

    if cast_next:
        next_wi_bf_ref[...] = next_wi_ref[...].astype(BF16)
        next_wo_bf_ref[...] = (0.5 * next_wo_ref[...]).astype(BF16)


def _ffn_half(x, gain, w_in, w_out, next_weights=None, *, tm, tf):
    m, d = x.shape
    d_ff = w_out.shape[0]
    n_ff = d_ff // tf
    n_rows = m // tm
    est = (2 * 2 * tm * d * 4 + tm * d * 2 + 2 * (2 * d * tf + tf * d) * 2
           + 3 * tm * tf * 4)
    in_specs = [
        pl.BlockSpec((tm, d), lambda i, j: (i, 0)),
        pl.BlockSpec((1, d), lambda i, j: (0, 0)),
        pl.BlockSpec((d, tf), lambda i, j: (0, j)),
        pl.BlockSpec((d, tf), lambda i, j: (0, n_ff + j)),
        pl.BlockSpec((tf, d), lambda i, j: (j, 0)),
    ]
    operands = [x, gain.reshape(1, d), w_in, w_in, w_out]
    out_shape = [jax.ShapeDtypeStruct((m, d), F32)]
    out_specs = [pl.BlockSpec((tm, d), lambda i, j: (i, 0))]
    if next_weights is not None:
        next_wi, next_wo, layer, half = next_weights
        wi_rows, wi_cols = d // n_rows, 2 * d_ff // n_ff
        wo_rows = d_ff // (n_rows * n_ff)
        assert wi_rows * n_rows == d and wi_cols * n_ff == 2 * d_ff
        assert wo_rows * n_rows * n_ff == d_ff
        assert wi_rows % 16 == 0 and wo_rows % 16 == 0 and wi_cols % V7X_LANES == 0
        in_specs += [
            pl.BlockSpec((None, None, wi_rows, wi_cols), lambda i, j: (layer, half, i, j)),
            pl.BlockSpec((None, None, wo_rows, d), lambda i, j: (layer, half, i * n_ff + j, 0)),
        ]
        operands += [next_wi, next_wo]
        out_shape += [jax.ShapeDtypeStruct((d, 2 * d_ff), BF16),
                      jax.ShapeDtypeStruct((d_ff, d), BF16)]
        out_specs += [pl.BlockSpec((wi_rows, wi_cols), lambda i, j: (i, j)),
                      pl.BlockSpec((wo_rows, d), lambda i, j: (i * n_ff + j, 0))]
        est += 2 * (wi_rows * wi_cols + wo_rows * d) * (4 + 2)
    outs = pl.pallas_call(
        functools.partial(_ffn_kernel, cast_next=next_weights is not None),
        out_shape=out_shape,
        grid=(n_rows, n_ff),
        in_specs=in_specs,
        out_specs=out_specs,
        scratch_shapes=[pltpu.VMEM((tm, d), BF16)],
        compiler_params=pltpu.CompilerParams(
            dimension_semantics=("parallel", "arbitrary"),
            vmem_limit_bytes=_vmem_limit(est)),
        name="ffn_half",
    )(*operands)
    return outs[0] if next_weights is None else tuple(outs)


def _pool_kernel(x_ref, halo_ref, first_halo_ref, g_ref, w_ref, s_ref, o_ref,
                 ext_ref, *, tm, tiles_per_seq, pos0):
    i = pl.program_id(0)
    t = i % tiles_per_seq
    gain = g_ref[...]
    x = x_ref[...]
    halo = jnp.where(t == 0, first_halo_ref[...], halo_ref[...])
    ext_ref[0:MAX_WINDOW, :] = _rms_norm_rows(halo, gain)
    ext_ref[MAX_WINDOW:, :] = _rms_norm_rows(x, gain)
    pos = pos0 + t * tm + lax.broadcasted_iota(jnp.int32, (tm, 1), 0)
    group = x.shape[1] // len(POOL_WINDOWS)
    for gi, w in enumerate(POOL_WINDOWS):
        assert w & (w - 1) == 0 and w <= MAX_WINDOW
        cols = slice(gi * group, (gi + 1) * group)
        tot = ext_ref[:, cols]
        step = 1
        while step < w:
            tot = tot + pltpu.roll(tot, step, axis=0)
            step *= 2
        tot = tot[MAX_WINDOW:]
        hn = ext_ref[MAX_WINDOW:, cols]
        cnt = jnp.minimum(pos + 1, w).astype(F32)
        y = (tot / cnt - hn).astype(BF16)
        mixed = jnp.dot(y, w_ref[gi], preferred_element_type=F32)
        o_ref[:, cols] = x[:, cols] + s_ref[:, cols] * mixed


def _pool_mixer(x, first_halo, gain, w_pool, scale, *, tm, seq, pos0):
    m, d = x.shape
    tiles_per_seq = seq // tm
    halo_blocks = tm // MAX_WINDOW
    est = 2 * 2 * tm * d * 4 + (tm + MAX_WINDOW) * d * 4 + 2 * w_pool.size * 2
    kern = functools.partial(_pool_kernel, tm=tm, tiles_per_seq=tiles_per_seq, pos0=pos0)
    return pl.pallas_call(
        kern,
        out_shape=jax.ShapeDtypeStruct((m, d), F32),
        grid=(m // tm,),
        in_specs=[
            pl.BlockSpec((tm, d), lambda i: (i, 0)),
            pl.BlockSpec((MAX_WINDOW, d), lambda i: (jnp.maximum(i * halo_blocks - 1, 0), 0)),
            pl.BlockSpec((MAX_WINDOW, d), lambda i: (0, 0)),
            pl.BlockSpec((1, d), lambda i: (0, 0)),
            pl.BlockSpec(w_pool.shape, lambda i: (0, 0, 0)),
            pl.BlockSpec((1, d), lambda i: (0, 0)),
        ],
        out_specs=pl.BlockSpec((tm, d), lambda i: (i, 0)),
        scratch_shapes=[pltpu.VMEM((tm + MAX_WINDOW, d), F32)],
        compiler_params=pltpu.CompilerParams(
            dimension_semantics=("arbitrary",),
            vmem_limit_bytes=_vmem_limit(est)),
        name="pool_mixer",
    )(x, x, first_halo, gain.reshape(1, d), w_pool, scale.reshape(1, d))


def _qkv_kernel(x_ref, g_ref, w_ref, cg_ref, o_ref, xn_ref, *, tn, d):
    j = pl.program_id(1)

    @pl.when(j == 0)
    def _():
        xn_ref[...] = _rms_norm_rows(x_ref[...], g_ref[...]).astype(BF16)

    res = jnp.dot(xn_ref[...], w_ref[...], preferred_element_type=F32)
    is_v = j >= 2 * (d // tn)
    for hh in range(tn // HEAD_DIM):
        cols = slice(hh * HEAD_DIM, (hh + 1) * HEAD_DIM)
        blk = res[:, cols]
        ms = jnp.mean(blk * blk, axis=-1, keepdims=True)
        inv = jnp.where(is_v, 1.0, lax.rsqrt(ms + RMS_EPS))
        o_ref[:, cols] = (blk * inv * cg_ref[:, cols]).astype(o_ref.dtype)


def _qkv_proj(x, gain, w_qkv, qk_norm, *, tm, tn):
    m, d = x.shape
    n = w_qkv.shape[1]
    n_heads = d // HEAD_DIM
    col_gain = jnp.concatenate([
        jnp.tile(qk_norm[0].astype(F32) * (HEAD_DIM ** -0.5 * LOG2E), n_heads),
        jnp.tile(qk_norm[1].astype(F32), n_heads),
        jnp.ones((d,), F32)]).reshape(1, n)
    est = 2 * tm * d * 4 + tm * d * 2 + 2 * d * tn * 2 + 2 * tm * tn * 2 + tm * tn * 4
    kern = functools.partial(_qkv_kernel, tn=tn, d=d)
    return pl.pallas_call(
        kern,
        out_shape=jax.ShapeDtypeStruct((m, n), BF16),
        grid=(m // tm, n // tn),
        in_specs=[
            pl.BlockSpec((tm, d), lambda i, j: (i, 0)),
            pl.BlockSpec((1, d), lambda i, j: (0, 0)),
            pl.BlockSpec((d, tn), lambda i, j: (0, j)),
            pl.BlockSpec((1, tn), lambda i, j: (0, j)),
        ],
        out_specs=pl.BlockSpec((tm, tn), lambda i, j: (i, j)),
        scratch_shapes=[pltpu.VMEM((tm, d), BF16)],
        compiler_params=pltpu.CompilerParams(
            dimension_semantics=("parallel", "arbitrary"),
            vmem_limit_bytes=_vmem_limit(est)),
        name="qkv_proj",
    )(x, gain.reshape(1, d), w_qkv, col_gain)


MASKED_LOGIT = -1e30
PRUNE_LOG2 = -150.0


def _sb_chunk(q, k_blk, v_blk, k_mask, tri, r_ref, acc_ref, sub):
    n_sub = k_blk.shape[0] // sub
    z = lax.dot_general(q, k_blk, (((1,), (1,)), ((), ())), preferred_element_type=F32)
    if k_mask is not None:
        z = jnp.where(k_mask, z, MASKED_LOGIT)
    neg_abs = pltpu.bitcast(pltpu.bitcast(z, jnp.uint32) | jnp.uint32(0x80000000), F32)
    neg_log_stay = jnp.maximum(z, 0.0) + jnp.log(1.0 + jnp.exp2(neg_abs)) * LOG2E
    nls_bf16 = neg_log_stay.astype(BF16)
    r = r_ref[...]
    parts = [None] * n_sub
    for j in range(n_sub - 1, -1, -1):
        cols = slice(j * sub, (j + 1) * sub)
        log_a = (jnp.dot(nls_bf16[:, cols], tri, preferred_element_type=F32)
                 + z[:, cols] + jnp.concatenate([r] * (sub // V7X_LANES), axis=1))
        parts[j] = jnp.exp2(log_a).astype(BF16)
        r = r - jnp.sum(neg_log_stay[:, cols], axis=1, keepdims=True)
    a = parts[0] if n_sub == 1 else jnp.concatenate(parts, axis=1)
    acc_ref[...] += jnp.dot(a, v_blk, preferred_element_type=F32)
    r_ref[...] = r


def _sb_kernel(q_ref, k_ref, v_ref, km_ref, vm_ref, tri_ref, o_ref, r_ref, acc_ref,
               *, tq, n_meta, heads):
    qi = pl.program_id(2)
    tri = tri_ref[...]
    r_ref[...] = jnp.zeros_like(r_ref)
    acc_ref[...] = jnp.zeros_like(acc_ref)
    head_cols = [slice(hh * HEAD_DIM, (hh + 1) * HEAD_DIM) for hh in range(heads)]

    first = jnp.maximum(qi - 1, 0)
    start = pl.multiple_of(first * tq, tq)
    col_minus_row = (lax.broadcasted_iota(jnp.int32, (tq, 2 * tq), 1)
                     - lax.broadcasted_iota(jnp.int32, (tq, 2 * tq), 0))
    causal = col_minus_row < (qi - first) * tq
    for hh, cols in enumerate(head_cols):
        _sb_chunk(q_ref[:, cols], k_ref[pl.ds(start, 2 * tq), cols],
                  v_ref[pl.ds(start, 2 * tq), cols], causal, tri,
                  r_ref.at[hh], acc_ref.at[hh], tq)

    def more_keys_matter(state):
        n_done, r_max = state
        return jnp.logical_and(n_done < first, r_max > PRUNE_LOG2)

    def earlier_chunk(state):
        n_done, _ = state
        start = pl.multiple_of((first - 1 - n_done) * tq, tq)
        for hh, cols in enumerate(head_cols):
            _sb_chunk(q_ref[:, cols], k_ref[pl.ds(start, tq), cols],
                      v_ref[pl.ds(start, tq), cols], None, tri,
                      r_ref.at[hh], acc_ref.at[hh], tq)
        return n_done + 1, jnp.max(r_ref[...])

    _, r_max = lax.while_loop(more_keys_matter, earlier_chunk,
                              (jnp.int32(0), jnp.max(r_ref[...])))

    @pl.when(r_max > PRUNE_LOG2)
    def _():
        n_pad = km_ref.shape[0]
        is_meta = lax.broadcasted_iota(jnp.int32, (tq, n_pad), 1) < n_meta
        for hh, cols in enumerate(head_cols):
            _sb_chunk(q_ref[:, cols], km_ref[:, cols], vm_ref[:, cols], is_meta,
                      tri_ref[0:n_pad, 0:n_pad], r_ref.at[hh], acc_ref.at[hh], n_pad)

    for hh, cols in enumerate(head_cols):
        o_ref[:, cols] = acc_ref[hh].astype(o_ref.dtype)


def _sb_attention(qkv, meta_kv, *, n_batch, seq, n_heads, n_meta, tq, heads):
    m = qkv.shape[0]
    d = n_heads * HEAD_DIM
    nq = seq // tq
    n_pad = meta_kv.shape[0]
    assert seq >= 2 * tq and n_pad <= tq and n_heads % heads == 0
    groups = n_heads // heads
    width = heads * HEAD_DIM
    tri = jnp.where(lax.broadcasted_iota(jnp.int32, (tq, tq), 0)
                    >= lax.broadcasted_iota(jnp.int32, (tq, tq), 1), -1.0, 0.0).astype(BF16)
    est = (2 * 2 * tq * width * 2 + 2 * 2 * seq * width * 2
           + 2 * 2 * n_pad * width * 2 + 2 * tq * tq * 2
           + heads * tq * (V7X_LANES + HEAD_DIM) * 4 + heads * 8 * tq * 2 * tq * 4)
    kern = functools.partial(_sb_kernel, tq=tq, n_meta=n_meta, heads=heads)
    return pl.pallas_call(
        kern,
        out_shape=jax.ShapeDtypeStruct((m, d), BF16),
        grid=(n_batch, groups, nq),
        in_specs=[
            pl.BlockSpec((tq, width), lambda b, g, i: (b * nq + i, g)),
            pl.BlockSpec((seq, width), lambda b, g, i: (b, groups + g)),
            pl.BlockSpec((seq, width), lambda b, g, i: (b, 2 * groups + g)),
            pl.BlockSpec((n_pad, width), lambda b, g, i: (0, groups + g)),
            pl.BlockSpec((n_pad, width), lambda b, g, i: (0, 2 * groups + g)),
            pl.BlockSpec((tq, tq), lambda b, g, i: (0, 0)),
        ],
        out_specs=pl.BlockSpec((tq, width), lambda b, g, i: (b * nq + i, g)),
        scratch_shapes=[pltpu.VMEM((heads, tq, V7X_LANES), F32),
                        pltpu.VMEM((heads, tq, HEAD_DIM), F32)],
        compiler_params=pltpu.CompilerParams(
            dimension_semantics=("parallel", "parallel", "arbitrary"),
            vmem_limit_bytes=_vmem_limit(est)),
        name="stick_breaking_attention",
    )(qkv, qkv, qkv, meta_kv, meta_kv, tri)


def _oproj_kernel(x_ref, o_ref, w_ref, out_ref):
    out_ref[...] = x_ref[...] + jnp.dot(o_ref[...], w_ref[...], preferred_element_type=F32)


def _out_proj(x, o, w_o, *, tm, tn):
    m, d = x.shape
    est = 2 * (tm * tn * 4 * 2 + tm * d * 2 + d * tn * 2)
    return pl.pallas_call(
        _oproj_kernel,
        out_shape=jax.ShapeDtypeStruct((m, d), F32),
        grid=(m // tm, d // tn),
        in_specs=[
            pl.BlockSpec((tm, tn), lambda i, j: (i, j)),
            pl.BlockSpec((tm, d), lambda i, j: (i, 0)),
            pl.BlockSpec((d, tn), lambda i, j: (0, j)),
        ],
        out_specs=pl.BlockSpec((tm, tn), lambda i, j: (i, j)),
        compiler_params=pltpu.CompilerParams(
            dimension_semantics=("parallel", "arbitrary"),
            vmem_limit_bytes=_vmem_limit(est)),
        name="attn_out_proj",
    )(x, o, w_o)


def kernel(x, meta, ffn_norm, ffn_w_in, ffn_w_out, mix_norm, pool_w, pool_scale,
           sb_w_qkv, sb_qk_norm, sb_w_o):
    n_batch, seq, d = x.shape
    n_meta = meta.shape[0]
    d_ff = ffn_w_out.shape[2]
    n_heads = d // HEAD_DIM
    assert n_meta == MAX_WINDOW and seq % 512 == 0 and d_ff % 512 == 0

    w00 = (ffn_w_in[0, 0].astype(BF16), (0.5 * ffn_w_out[0, 0]).astype(BF16))
    w_pool = pool_w[0].astype(BF16)
    w_qkv = sb_w_qkv[0].astype(BF16)
    w_o = sb_w_o[0].astype(BF16)

    def ffn(h, layer, half, weights, tm, next_sublayer=None):
        nxt = None if next_sublayer is None else (ffn_w_in, ffn_w_out) + next_sublayer
        return _ffn_half(h, ffn_norm[layer, half], weights[0], weights[1], nxt, tm=tm, tf=512)

    h = x.reshape(n_batch * seq, d)
    h, *w01 = ffn(h, 0, 0, w00, FFN_ROW_TILE, next_sublayer=(0, 1))

    hm = ffn(meta.astype(F32), 0, 0, w00, n_meta)
    hm_pre_pool = hm
    hm = _pool_mixer(hm, jnp.zeros_like(hm), mix_norm[0], w_pool, pool_scale[0],
                     tm=n_meta, seq=n_meta, pos0=0)

    h = _pool_mixer(h, hm_pre_pool, mix_norm[0], w_pool, pool_scale[0],
                    tm=512, seq=seq, pos0=n_meta)
    h, *w10 = ffn(h, 0, 1, w01, FFN_ROW_TILE, next_sublayer=(1, 0))
    h, *w11 = ffn(h, 1, 0, w10, FFN_ROW_TILE, next_sublayer=(1, 1))

    hm = ffn(hm, 0, 1, w01, n_meta)
    hm = ffn(hm, 1, 0, w10, n_meta)
    meta_qkv = _qkv_proj(hm, mix_norm[1], w_qkv, sb_qk_norm[0], tm=n_meta, tn=512)
    meta_kv = jnp.pad(meta_qkv, ((0, V7X_LANES - n_meta), (0, 0)))

    qkv = _qkv_proj(h, mix_norm[1], w_qkv, sb_qk_norm[0], tm=512, tn=2048)
    o = _sb_attention(qkv, meta_kv, n_batch=n_batch, seq=seq, n_heads=n_heads,
                      n_meta=n_meta, tq=256, heads=4)
    h = _out_proj(h, o, w_o, tm=512, tn=2048)
    h = ffn(h, 1, 1, w11, FFN_ROW_TILE)
    return h.reshape(n_batch, seq, d)
```

```python
import functools
import math

import jax
import jax.numpy as jnp
from jax import lax
from jax.experimental import pallas as pl
from jax.experimental.pallas import tpu as pltpu

RMS_EPS = 1e-6
POOL_WINDOWS = (2, 4, 8, 16)
MAX_WINDOW = max(POOL_WINDOWS)
HEAD_DIM = 128
LOG2E = math.log2(math.e)

V7X_LANES = 128
V7X_VMEM_BYTES = 64 * 1024 * 1024

F32 = jnp.float32
BF16 = jnp.bfloat16

FFN_ROW_TILE = 1024


def _vmem_limit(estimate_bytes):
    return int(min(estimate_bytes * 1.25 + (4 << 20), V7X_VMEM_BYTES * 0.9))


def _rms_norm_rows(x, gain):
    ms = jnp.mean(x * x, axis=-1, keepdims=True)
    return x * lax.rsqrt(ms + RMS_EPS) * gain


def _ffn_kernel(*refs, cast_next):
    if cast_next:
        (x_ref, g_ref, wg_ref, wu_ref, wo_ref, next_wi_ref, next_wo_ref,
         o_ref, next_wi_bf_ref, next_wo_bf_ref, xn_ref) = refs
    else:
        x_ref, g_ref, wg_ref, wu_ref, wo_ref, o_ref, xn_ref = refs
    j = pl.program_id(1)

    @pl.when(j == 0)
    def _():
        x = x_ref[...]
        xn_ref[...] = _rms_norm_rows(x, g_ref[...]).astype(BF16)
        o_ref[...] = x

    xn = xn_ref[...]
    gate = jnp.dot(xn, wg_ref[...], preferred_element_type=F32)
    up = jnp.dot(xn, wu_ref[...], preferred_element_type=F32)
    act = (gate * jax.nn.sigmoid(gate) * up).astype(BF16)
    o_ref[...] += jnp.dot(act, wo_ref[...], preferred_element_type=F32)

    if cast_next:
        next_wi_bf_ref[...] = next_wi_ref[...].astype(BF16)
        next_wo_bf_ref[...] = (0.5 * next_wo_ref[...]).astype(BF16)


def _ffn_half(x, gain, w_in, w_out, next_weights=None, *, tm, tf):
    m, d = x.shape
    d_ff = w_out.shape[0]
    n_ff = d_ff // tf
    n_rows = m // tm
    est = (2 * 2 * tm * d * 4 + tm * d * 2 + 2 * (2 * d * tf + tf * d) * 2
           + 3 * tm * tf * 4)
    in_specs = [
        pl.BlockSpec((tm, d), lambda i, j: (i, 0)),
        pl.BlockSpec((1, d), lambda i, j: (0, 0)),
        pl.BlockSpec((d, tf), lambda i, j: (0, j)),
        pl.BlockSpec((d, tf), lambda i, j: (0, n_ff + j)),
        pl.BlockSpec((tf, d), lambda i, j: (j, 0)),
    ]
    operands = [x, gain.reshape(1, d), w_in, w_in, w_out]
    out_shape = [jax.ShapeDtypeStruct((m, d), F32)]
    out_specs = [pl.BlockSpec((tm, d), lambda i, j: (i, 0))]
    if next_weights is not None:
        next_wi, next_wo, layer, half = next_weights
        wi_rows, wi_cols = d // n_rows, 2 * d_ff // n_ff
        wo_rows = d_ff // (n_rows * n_ff)
        assert wi_rows * n_rows == d and wi_cols * n_ff == 2 * d_ff
        assert wo_rows * n_rows * n_ff == d_ff
        assert wi_rows % 16 == 0 and wo_rows % 16 == 0 and wi_cols % V7X_LANES == 0
        in_specs += [
            pl.BlockSpec((None, None, wi_rows, wi_cols), lambda i, j: (layer, half, i, j)),
            pl.BlockSpec((None, None, wo_rows, d), lambda i, j: (layer, half, i * n_ff + j, 0)),
        ]
        operands += [next_wi, next_wo]
        out_shape += [jax.ShapeDtypeStruct((d, 2 * d_ff), BF16),
                      jax.ShapeDtypeStruct((d_ff, d), BF16)]
        out_specs += [pl.BlockSpec((wi_rows, wi_cols), lambda i, j: (i, j)),
                      pl.BlockSpec((wo_rows, d), lambda i, j: (i * n_ff + j, 0))]
        est += 2 * (wi_rows * wi_cols + wo_rows * d) * (4 + 2)
    outs = pl.pallas_call(
        functools.partial(_ffn_kernel, cast_next=next_weights is not None),
        out_shape=out_shape,
        grid=(n_rows, n_ff),
        in_specs=in_specs,
        out_specs=out_specs,
        scratch_shapes=[pltpu.VMEM((tm, d), BF16)],
        compiler_params=pltpu.CompilerParams(
            dimension_semantics=("parallel", "arbitrary"),
            vmem_limit_bytes=_vmem_limit(est)),
        name="ffn_half",
    )(*operands)
    return outs[0] if next_weights is None else tuple(outs)


def _pool_kernel(x_ref, halo_ref, first_halo_ref, g_ref, w_ref, s_ref, o_ref,
                 ext_ref, *, tm, tiles_per_seq, pos0):
    i = pl.program_id(0)
    t = i % tiles_per_seq
    gain = g_ref[...]
    x = x_ref[...]
    halo = jnp.where(t == 0, first_halo_ref[...], halo_ref[...])
    ext_ref[0:MAX_WINDOW, :] = _rms_norm_rows(halo, gain)
    ext_ref[MAX_WINDOW:, :] = _rms_norm_rows(x, gain)
    pos = pos0 + t * tm + lax.broadcasted_iota(jnp.int32, (tm, 1), 0)
    group = x.shape[1] // len(POOL_WINDOWS)
    for gi, w in enumerate(POOL_WINDOWS):
        assert w & (w - 1) == 0 and w <= MAX_WINDOW
        cols = slice(gi * group, (gi + 1) * group)
        tot = ext_ref[:, cols]
        step = 1
        while step < w:
            tot = tot + pltpu.roll(tot, step, axis=0)
            step *= 2
        tot = tot[MAX_WINDOW:]
        hn = ext_ref[MAX_WINDOW:, cols]
        cnt = jnp.minimum(pos + 1, w).astype(F32)
        y = (tot / cnt - hn).astype(BF16)
        mixed = jnp.dot(y, w_ref[gi], preferred_element_type=F32)
        o_ref[:, cols] = x[:, cols] + s_ref[:, cols] * mixed


def _pool_mixer(x, first_halo, gain, w_pool, scale, *, tm, seq, pos0):
    m, d = x.shape
    tiles_per_seq = seq // tm
    halo_blocks = tm // MAX_WINDOW
    est = 2 * 2 * tm * d * 4 + (tm + MAX_WINDOW) * d * 4 + 2 * w_pool.size * 2
    kern = functools.partial(_pool_kernel, tm=tm, tiles_per_seq=tiles_per_seq, pos0=pos0)
    return pl.pallas_call(
        kern,
        out_shape=jax.ShapeDtypeStruct((m, d), F32),
        grid=(m // tm,),
        in_specs=[
            pl.BlockSpec((tm, d), lambda i: (i, 0)),
            pl.BlockSpec((MAX_WINDOW, d), lambda i: (jnp.maximum(i * halo_blocks - 1, 0), 0)),
            pl.BlockSpec((MAX_WINDOW, d), lambda i: (0, 0)),
            pl.BlockSpec((1, d), lambda i: (0, 0)),
            pl.BlockSpec(w_pool.shape, lambda i: (0, 0, 0)),
            pl.BlockSpec((1, d), lambda i: (0, 0)),
        ],
        out_specs=pl.BlockSpec((tm, d), lambda i: (i, 0)),
        scratch_shapes=[pltpu.VMEM((tm + MAX_WINDOW, d), F32)],
        compiler_params=pltpu.CompilerParams(
            dimension_semantics=("arbitrary",),
            vmem_limit_bytes=_vmem_limit(est)),
        name="pool_mixer",
    )(x, x, first_halo, gain.reshape(1, d), w_pool, scale.reshape(1, d))


def _qkv_kernel(x_ref, g_ref, w_ref, cg_ref, o_ref, xn_ref, *, tn, d):
    j = pl.program_id(1)

    @pl.when(j == 0)
    def _():
        xn_ref[...] = _rms_norm_rows(x_ref[...], g_ref[...]).astype(BF16)

    res = jnp.dot(xn_ref[...], w_ref[...], preferred_element_type=F32)
    is_v = j >= 2 * (d // tn)
    for hh in range(tn // HEAD_DIM):
        cols = slice(hh * HEAD_DIM, (hh + 1) * HEAD_DIM)
        blk = res[:, cols]
        ms = jnp.mean(blk * blk, axis=-1, keepdims=True)
        inv = jnp.where(is_v, 1.0, lax.rsqrt(ms + RMS_EPS))
        o_ref[:, cols] = (blk * inv * cg_ref[:, cols]).astype(o_ref.dtype)


def _qkv_proj(x, gain, w_qkv, qk_norm, *, tm, tn):
    m, d = x.shape
    n = w_qkv.shape[1]
    n_heads = d // HEAD_DIM
    col_gain = jnp.concatenate([
        jnp.tile(qk_norm[0].astype(F32) * (HEAD_DIM ** -0.5 * LOG2E), n_heads),
        jnp.tile(qk_norm[1].astype(F32), n_heads),
        jnp.ones((d,), F32)]).reshape(1, n)
    est = 2 * tm * d * 4 + tm * d * 2 + 2 * d * tn * 2 + 2 * tm * tn * 2 + tm * tn * 4
    kern = functools.partial(_qkv_kernel, tn=tn, d=d)
    return pl.pallas_call(
        kern,
        out_shape=jax.ShapeDtypeStruct((m, n), BF16),
        grid=(m // tm, n // tn),
        in_specs=[
            pl.BlockSpec((tm, d), lambda i, j: (i, 0)),
            pl.BlockSpec((1, d), lambda i, j: (0, 0)),
            pl.BlockSpec((d, tn), lambda i, j: (0, j)),
            pl.BlockSpec((1, tn), lambda i, j: (0, j)),
        ],
        out_specs=pl.BlockSpec((tm, tn), lambda i, j: (i, j)),
        scratch_shapes=[pltpu.VMEM((tm, d), BF16)],
        compiler_params=pltpu.CompilerParams(
            dimension_semantics=("parallel", "arbitrary"),
            vmem_limit_bytes=_vmem_limit(est)),
        name="qkv_proj",
    )(x, gain.reshape(1, d), w_qkv, col_gain)


MASKED_LOGIT = -1e30
PRUNE_LOG2 = -150.0


def _sb_scores(q, k_blk, k_mask):
    z = lax.dot_general(q, k_blk, (((1,), (1,)), ((), ())), preferred_element_type=F32)
    if k_mask is not None:
        z = jnp.where(k_mask, z, MASKED_LOGIT)
    neg_abs = pltpu.bitcast(pltpu.bitcast(z, jnp.uint32) | jnp.uint32(0x80000000), F32)
    neg_log_stay = jnp.maximum(z, 0.0) + jnp.log(1.0 + jnp.exp2(neg_abs)) * LOG2E
    return z, neg_log_stay, neg_log_stay.astype(BF16)


def _sb_weights(scores, tri, v_blk, r, sub):
    z, neg_log_stay, nls_bf16 = scores
    n_sub = z.shape[1] // sub
    parts = [None] * n_sub
    for j in range(n_sub - 1, -1, -1):
        cols = slice(j * sub, (j + 1) * sub)
        log_a = jnp.dot(nls_bf16[:, cols], tri, preferred_element_type=F32) + z[:, cols]
        tot = jnp.broadcast_to(jnp.sum(neg_log_stay[:, cols], axis=1, keepdims=True),
                               (z.shape[0], V7X_LANES))
        if r is None:
            r = -tot
        else:
            log_a = log_a + jnp.concatenate([r] * (sub // V7X_LANES), axis=1)
            r = r - tot
        parts[j] = jnp.exp2(log_a).astype(BF16)
    a = parts[0] if n_sub == 1 else jnp.concatenate(parts, axis=1)
    return jnp.dot(a, v_blk, preferred_element_type=F32), r


def _sb_chunk(q, k_blk, v_blk, k_mask, tri, r_ref, acc_ref, sub):
    out, r = _sb_weights(_sb_scores(q, k_blk, k_mask), tri, v_blk, r_ref[...], sub)
    acc_ref[...] += out
    r_ref[...] = r


def _sb_kernel(q_ref, k_ref, v_ref, km_ref, vm_ref, tri_ref, o_ref, r_ref, acc_ref,
               *, tq, n_meta, heads, qblocks):
    tri = tri_ref[...]
    head_cols = [slice(hh * HEAD_DIM, (hh + 1) * HEAD_DIM) for hh in range(heads)]
    col_minus_row = (lax.broadcasted_iota(jnp.int32, (tq, 2 * tq), 1)
                     - lax.broadcasted_iota(jnp.int32, (tq, 2 * tq), 0))
    rows = [slice(qb * tq, (qb + 1) * tq) for qb in range(qblocks)]
    qi = [pl.program_id(2) * qblocks + qb for qb in range(qblocks)]
    first = [jnp.maximum(q - 1, 0) for q in qi]

    scores = {}
    for qb in range(qblocks):
        start = pl.multiple_of(first[qb] * tq, tq)
        causal = col_minus_row < (qi[qb] - first[qb]) * tq
        for hh, cols in enumerate(head_cols):
            scores[qb, hh] = _sb_scores(q_ref[rows[qb], cols],
                                        k_ref[pl.ds(start, 2 * tq), cols], causal)
    for qb in range(qblocks):
        start = pl.multiple_of(first[qb] * tq, tq)
        for hh, cols in enumerate(head_cols):
            out, r = _sb_weights(scores[qb, hh], tri, v_ref[pl.ds(start, 2 * tq), cols],
                                 None, tq)
            acc_ref[qb, hh] = out
            r_ref[qb, hh] = r

    for qb in range(qblocks):
        def more_keys_matter(state, qb=qb):
            n_done, r_max = state
            return jnp.logical_and(n_done < first[qb], r_max > PRUNE_LOG2)

        def earlier_chunk(state, qb=qb):
            n_done, _ = state
            start = pl.multiple_of((first[qb] - 1 - n_done) * tq, tq)
            for hh, cols in enumerate(head_cols):
                _sb_chunk(q_ref[rows[qb], cols], k_ref[pl.ds(start, tq), cols],
                          v_ref[pl.ds(start, tq), cols], None, tri,
                          r_ref.at[qb, hh], acc_ref.at[qb, hh], tq)
            return n_done + 1, jnp.max(r_ref[qb])

        _, r_max = lax.while_loop(more_keys_matter, earlier_chunk,
                                  (jnp.int32(0), jnp.max(r_ref[qb])))

        @pl.when(r_max > PRUNE_LOG2)
        def _(qb=qb):
            n_pad = km_ref.shape[0]
            is_meta = lax.broadcasted_iota(jnp.int32, (tq, n_pad), 1) < n_meta
            for hh, cols in enumerate(head_cols):
                _sb_chunk(q_ref[rows[qb], cols], km_ref[:, cols], vm_ref[:, cols], is_meta,
                          tri_ref[0:n_pad, 0:n_pad], r_ref.at[qb, hh], acc_ref.at[qb, hh],
                          n_pad)

        for hh, cols in enumerate(head_cols):
            o_ref[rows[qb], cols] = acc_ref[qb, hh].astype(o_ref.dtype)


def _sb_attention(qkv, meta_kv, *, n_batch, seq, n_heads, n_meta, tq, heads, qblocks):
    m = qkv.shape[0]
    d = n_heads * HEAD_DIM
    n_pad = meta_kv.shape[0]
    tstep = qblocks * tq
    nq = seq // tstep
    assert seq % tstep == 0 and seq >= 2 * tq and n_pad <= tq and n_heads % heads == 0
    groups = n_heads // heads
    width = heads * HEAD_DIM
    tri = jnp.where(lax.broadcasted_iota(jnp.int32, (tq, tq), 0)
                    >= lax.broadcasted_iota(jnp.int32, (tq, tq), 1), -1.0, 0.0).astype(BF16)
    pairs = qblocks * heads
    est = (2 * 2 * tstep * width * 2 + 2 * 2 * seq * width * 2
           + 2 * 2 * n_pad * width * 2 + 2 * tq * tq * 2
           + pairs * tq * (V7X_LANES + HEAD_DIM) * 4 + pairs * 4 * tq * 2 * tq * 4)
    kern = functools.partial(_sb_kernel, tq=tq, n_meta=n_meta, heads=heads, qblocks=qblocks)
    return pl.pallas_call(
        kern,
        out_shape=jax.ShapeDtypeStruct((m, d), BF16),
        grid=(n_batch, groups, nq),
        in_specs=[
            pl.BlockSpec((tstep, width), lambda b, g, i: (b * nq + i, g)),
            pl.BlockSpec((seq, width), lambda b, g, i: (b, groups + g)),
            pl.BlockSpec((seq, width), lambda b, g, i: (b, 2 * groups + g)),
            pl.BlockSpec((n_pad, width), lambda b, g, i: (0, groups + g)),
            pl.BlockSpec((n_pad, width), lambda b, g, i: (0, 2 * groups + g)),
            pl.BlockSpec((tq, tq), lambda b, g, i: (0, 0)),
        ],
        out_specs=pl.BlockSpec((tstep, width), lambda b, g, i: (b * nq + i, g)),
        scratch_shapes=[pltpu.VMEM((qblocks, heads, tq, V7X_LANES), F32),
                        pltpu.VMEM((qblocks, heads, tq, HEAD_DIM), F32)],
        compiler_params=pltpu.CompilerParams(
            dimension_semantics=("parallel", "parallel", "arbitrary"),
            vmem_limit_bytes=_vmem_limit(est)),
        name="stick_breaking_attention",
    )(qkv, qkv, qkv, meta_kv, meta_kv, tri)


def _oproj_kernel(x_ref, o_ref, w_ref, out_ref):
    out_ref[...] = x_ref[...] + jnp.dot(o_ref[...], w_ref[...], preferred_element_type=F32)


def _out_proj(x, o, w_o, *, tm, tn):
    m, d = x.shape
    est = 2 * (tm * tn * 4 * 2 + tm * d * 2 + d * tn * 2)
    return pl.pallas_call(
        _oproj_kernel,
        out_shape=jax.ShapeDtypeStruct((m, d), F32),
        grid=(m // tm, d // tn),
        in_specs=[
            pl.BlockSpec((tm, tn), lambda i, j: (i, j)),
            pl.BlockSpec((tm, d), lambda i, j: (i, 0)),
            pl.BlockSpec((d, tn), lambda i, j: (0, j)),
        ],
        out_specs=pl.BlockSpec((tm, tn), lambda i, j: (i, j)),
        compiler_params=pltpu.CompilerParams(
            dimension_semantics=("parallel", "arbitrary"),
            vmem_limit_bytes=_vmem_limit(est)),
        name="attn_out_proj",
    )(x, o, w_o)


def kernel(x, meta, ffn_norm, ffn_w_in, ffn_w_out, mix_norm, pool_w, pool_scale,
           sb_w_qkv, sb_qk_norm, sb_w_o):
    n_batch, seq, d = x.shape
    n_meta = meta.shape[0]
    d_ff = ffn_w_out.shape[2]
    n_heads = d // HEAD_DIM
    assert n_meta == MAX_WINDOW and seq % 512 == 0 and d_ff % 512 == 0

    w00 = (ffn_w_in[0, 0].astype(BF16), (0.5 * ffn_w_out[0, 0]).astype(BF16))
    w_pool = pool_w[0].astype(BF16)
    w_qkv = sb_w_qkv[0].astype(BF16)
    w_o = sb_w_o[0].astype(BF16)

    def ffn(h, layer, half, weights, tm, next_sublayer=None):
        nxt = None if next_sublayer is None else (ffn_w_in, ffn_w_out) + next_sublayer
        return _ffn_half(h, ffn_norm[layer, half], weights[0], weights[1], nxt, tm=tm, tf=512)

    h = x.reshape(n_batch * seq, d)
    h, *w01 = ffn(h, 0, 0, w00, FFN_ROW_TILE, next_sublayer=(0, 1))

    hm = ffn(meta.astype(F32), 0, 0, w00, n_meta)
    hm_pre_pool = hm
    hm = _pool_mixer(hm, jnp.zeros_like(hm), mix_norm[0], w_pool, pool_scale[0],
                     tm=n_meta, seq=n_meta, pos0=0)

    h = _pool_mixer(h, hm_pre_pool, mix_norm[0], w_pool, pool_scale[0],
                    tm=512, seq=seq, pos0=n_meta)
    h, *w10 = ffn(h, 0, 1, w01, FFN_ROW_TILE, next_sublayer=(1, 0))
    h, *w11 = ffn(h, 1, 0, w10, FFN_ROW_TILE, next_sublayer=(1, 1))

    hm = ffn(hm, 0, 1, w01, n_meta)
    hm = ffn(hm, 1, 0, w10, n_meta)
    meta_qkv = _qkv_proj(hm, mix_norm[1], w_qkv, sb_qk_norm[0], tm=n_meta, tn=512)
    meta_kv = jnp.pad(meta_qkv, ((0, V7X_LANES - n_meta), (0, 0)))

    qkv = _qkv_proj(h, mix_norm[1], w_qkv, sb_qk_norm[0], tm=512, tn=2048)
    o = _sb_attention(qkv, meta_kv, n_batch=n_batch, seq=seq, n_heads=n_heads,
                      n_meta=n_meta, tq=256, heads=4, qblocks=2)
    h = _out_proj(h, o, w_o, tm=512, tn=2048)
    h = ffn(h, 1, 1, w11, FFN_ROW_TILE)
    return h.reshape(n_batch, seq, d)
```

```python
import functools
import math

import jax
import jax.numpy as jnp
from jax import lax
from jax.experimental import pallas as pl
from jax.experimental.pallas import tpu as pltpu

RMS_EPS = 1e-6
POOL_WINDOWS = (2, 4, 8, 16)
MAX_WINDOW = max(POOL_WINDOWS)
HEAD_DIM = 128
LOG2E = math.log2(math.e)

V7X_LANES = 128
V7X_VMEM_BYTES = 64 * 1024 * 1024

F32 = jnp.float32
BF16 = jnp.bfloat16

FFN_ROW_TILE = 1024
FFN_FF_TILE = 512
ROW_TILE = 512
QKV_COL_TILE = 2048
META_QKV_COL_TILE = 512
OUT_PROJ_COL_TILE = 2048
ATTN_Q_ROWS = 256
ATTN_HEADS_PER_STEP = 4
ATTN_QBLOCKS_PER_STEP = 2

VMEM_ESTIMATE_MARGIN = 1.25
VMEM_TEMPORARIES_BYTES = 4 * 1024 * 1024
VMEM_MAX_FRACTION = 0.9


def _vmem_limit(estimate_bytes):
    return int(min(estimate_bytes * VMEM_ESTIMATE_MARGIN + VMEM_TEMPORARIES_BYTES,
                   V7X_VMEM_BYTES * VMEM_MAX_FRACTION))


def _rms_norm_rows(x, gain):
    ms = jnp.mean(x * x, axis=-1, keepdims=True)
    return x * lax.rsqrt(ms + RMS_EPS) * gain


def _ffn_kernel(*refs, cast_next):
    if cast_next:
        (x_ref, g_ref, wg_ref, wu_ref, wo_ref, next_wi_ref, next_wo_ref,
         o_ref, next_wi_bf_ref, next_wo_bf_ref, xn_ref) = refs
    else:
        x_ref, g_ref, wg_ref, wu_ref, wo_ref, o_ref, xn_ref = refs
    j = pl.program_id(1)

    @pl.when(j == 0)
    def _():
        x = x_ref[...]
        xn_ref[...] = _rms_norm_rows(x, g_ref[...]).astype(BF16)
        o_ref[...] = x

    xn = xn_ref[...]
    gate = jnp.dot(xn, wg_ref[...], preferred_element_type=F32)
    up = jnp.dot(xn, wu_ref[...], preferred_element_type=F32)
    act = (gate * jax.nn.sigmoid(gate) * up).astype(BF16)
    o_ref[...] += jnp.dot(act, wo_ref[...], preferred_element_type=F32)

    if cast_next:
        next_wi_bf_ref[...] = next_wi_ref[...].astype(BF16)
        next_wo_bf_ref[...] = (0.5 * next_wo_ref[...]).astype(BF16)


def _ffn_half(x, gain, w_in, w_out, next_weights=None, *, tm, tf):
    m, d = x.shape
    d_ff = w_out.shape[0]
    n_ff = d_ff // tf
    n_rows = m // tm
    est = (2 * 2 * tm * d * 4 + tm * d * 2 + 2 * (2 * d * tf + tf * d) * 2
           + 3 * tm * tf * 4)
    in_specs = [
        pl.BlockSpec((tm, d), lambda i, j: (i, 0)),
        pl.BlockSpec((1, d), lambda i, j: (0, 0)),
        pl.BlockSpec((d, tf), lambda i, j: (0, j)),
        pl.BlockSpec((d, tf), lambda i, j: (0, n_ff + j)),
        pl.BlockSpec((tf, d), lambda i, j: (j, 0)),
    ]
    operands = [x, gain.reshape(1, d), w_in, w_in, w_out]
    out_shape = [jax.ShapeDtypeStruct((m, d), F32)]
    out_specs = [pl.BlockSpec((tm, d), lambda i, j: (i, 0))]
    if next_weights is not None:
        next_wi, next_wo, layer, half = next_weights
        wi_rows, wi_cols = d // n_rows, 2 * d_ff // n_ff
        wo_rows = d_ff // (n_rows * n_ff)
        assert wi_rows * n_rows == d and wi_cols * n_ff == 2 * d_ff
        assert wo_rows * n_rows * n_ff == d_ff
        assert wi_rows % 16 == 0 and wo_rows % 16 == 0 and wi_cols % V7X_LANES == 0
        in_specs += [
            pl.BlockSpec((None, None, wi_rows, wi_cols), lambda i, j: (layer, half, i, j)),
            pl.BlockSpec((None, None, wo_rows, d), lambda i, j: (layer, half, i * n_ff + j, 0)),
        ]
        operands += [next_wi, next_wo]
        out_shape += [jax.ShapeDtypeStruct((d, 2 * d_ff), BF16),
                      jax.ShapeDtypeStruct((d_ff, d), BF16)]
        out_specs += [pl.BlockSpec((wi_rows, wi_cols), lambda i, j: (i, j)),
                      pl.BlockSpec((wo_rows, d), lambda i, j: (i * n_ff + j, 0))]
        est += 2 * (wi_rows * wi_cols + wo_rows * d) * (4 + 2)
    outs = pl.pallas_call(
        functools.partial(_ffn_kernel, cast_next=next_weights is not None),
        out_shape=out_shape,
        grid=(n_rows, n_ff),
        in_specs=in_specs,
        out_specs=out_specs,
        scratch_shapes=[pltpu.VMEM((tm, d), BF16)],
        compiler_params=pltpu.CompilerParams(
            dimension_semantics=("parallel", "arbitrary"),
            vmem_limit_bytes=_vmem_limit(est)),
        name="ffn_half",
    )(*operands)
    return outs[0] if next_weights is None else tuple(outs)


def _pool_kernel(x_ref, halo_ref, first_halo_ref, g_ref, w_ref, s_ref, o_ref,
                 ext_ref, *, tm, tiles_per_seq, pos0):
    i = pl.program_id(0)
    t = i % tiles_per_seq
    gain = g_ref[...]
    x = x_ref[...]
    halo = jnp.where(t == 0, first_halo_ref[...], halo_ref[...])
    ext_ref[0:MAX_WINDOW, :] = _rms_norm_rows(halo, gain)
    ext_ref[MAX_WINDOW:, :] = _rms_norm_rows(x, gain)
    pos = pos0 + t * tm + lax.broadcasted_iota(jnp.int32, (tm, 1), 0)
    group = x.shape[1] // len(POOL_WINDOWS)
    for gi, w in enumerate(POOL_WINDOWS):
        assert w & (w - 1) == 0 and w <= MAX_WINDOW
        cols = slice(gi * group, (gi + 1) * group)
        tot = ext_ref[:, cols]
        step = 1
        while step < w:
            tot = tot + pltpu.roll(tot, step, axis=0)
            step *= 2
        tot = tot[MAX_WINDOW:]
        hn = ext_ref[MAX_WINDOW:, cols]
        cnt = jnp.minimum(pos + 1, w).astype(F32)
        y = (tot / cnt - hn).astype(BF16)
        mixed = jnp.dot(y, w_ref[gi], preferred_element_type=F32)
        o_ref[:, cols] = x[:, cols] + s_ref[:, cols] * mixed


def _pool_mixer(x, first_halo, gain, w_pool, scale, *, tm, seq, pos0):
    m, d = x.shape
    tiles_per_seq = seq // tm
    halo_blocks = tm // MAX_WINDOW
    est = 2 * 2 * tm * d * 4 + (tm + MAX_WINDOW) * d * 4 + 2 * w_pool.size * 2
    kern = functools.partial(_pool_kernel, tm=tm, tiles_per_seq=tiles_per_seq, pos0=pos0)
    return pl.pallas_call(
        kern,
        out_shape=jax.ShapeDtypeStruct((m, d), F32),
        grid=(m // tm,),
        in_specs=[
            pl.BlockSpec((tm, d), lambda i: (i, 0)),
            pl.BlockSpec((MAX_WINDOW, d), lambda i: (jnp.maximum(i * halo_blocks - 1, 0), 0)),
            pl.BlockSpec((MAX_WINDOW, d), lambda i: (0, 0)),
            pl.BlockSpec((1, d), lambda i: (0, 0)),
            pl.BlockSpec(w_pool.shape, lambda i: (0, 0, 0)),
            pl.BlockSpec((1, d), lambda i: (0, 0)),
        ],
        out_specs=pl.BlockSpec((tm, d), lambda i: (i, 0)),
        scratch_shapes=[pltpu.VMEM((tm + MAX_WINDOW, d), F32)],
        compiler_params=pltpu.CompilerParams(
            dimension_semantics=("arbitrary",),
            vmem_limit_bytes=_vmem_limit(est)),
        name="pool_mixer",
    )(x, x, first_halo, gain.reshape(1, d), w_pool, scale.reshape(1, d))


def _qkv_kernel(x_ref, g_ref, w_ref, cg_ref, o_ref, xn_ref, *, tn, d):
    j = pl.program_id(1)

    @pl.when(j == 0)
    def _():
        xn_ref[...] = _rms_norm_rows(x_ref[...], g_ref[...]).astype(BF16)

    res = jnp.dot(xn_ref[...], w_ref[...], preferred_element_type=F32)
    is_v = j >= 2 * (d // tn)
    for hh in range(tn // HEAD_DIM):
        cols = slice(hh * HEAD_DIM, (hh + 1) * HEAD_DIM)
        blk = res[:, cols]
        ms = jnp.mean(blk * blk, axis=-1, keepdims=True)
        inv = jnp.where(is_v, 1.0, lax.rsqrt(ms + RMS_EPS))
        o_ref[:, cols] = (blk * inv * cg_ref[:, cols]).astype(o_ref.dtype)


def _qkv_proj(x, gain, w_qkv, qk_norm, *, tm, tn):
    m, d = x.shape
    n = w_qkv.shape[1]
    n_heads = d // HEAD_DIM
    col_gain = jnp.concatenate([
        jnp.tile(qk_norm[0].astype(F32) * (HEAD_DIM ** -0.5 * LOG2E), n_heads),
        jnp.tile(qk_norm[1].astype(F32), n_heads),
        jnp.ones((d,), F32)]).reshape(1, n)
    est = 2 * tm * d * 4 + tm * d * 2 + 2 * d * tn * 2 + 2 * tm * tn * 2 + tm * tn * 4
    kern = functools.partial(_qkv_kernel, tn=tn, d=d)
    return pl.pallas_call(
        kern,
        out_shape=jax.ShapeDtypeStruct((m, n), BF16),
        grid=(m // tm, n // tn),
        in_specs=[
            pl.BlockSpec((tm, d), lambda i, j: (i, 0)),
            pl.BlockSpec((1, d), lambda i, j: (0, 0)),
            pl.BlockSpec((d, tn), lambda i, j: (0, j)),
            pl.BlockSpec((1, tn), lambda i, j: (0, j)),
        ],
        out_specs=pl.BlockSpec((tm, tn), lambda i, j: (i, j)),
        scratch_shapes=[pltpu.VMEM((tm, d), BF16)],
        compiler_params=pltpu.CompilerParams(
            dimension_semantics=("parallel", "arbitrary"),
            vmem_limit_bytes=_vmem_limit(est)),
        name="qkv_proj",
    )(x, gain.reshape(1, d), w_qkv, col_gain)


MASKED_LOGIT = -1e30
PRUNE_LOG2 = -150.0


def _sb_scores(q, k_blk, k_mask):
    z = lax.dot_general(q, k_blk, (((1,), (1,)), ((), ())), preferred_element_type=F32)
    if k_mask is not None:
        z = jnp.where(k_mask, z, MASKED_LOGIT)
    neg_abs = pltpu.bitcast(pltpu.bitcast(z, jnp.uint32) | jnp.uint32(0x80000000), F32)
    neg_log_stay = jnp.maximum(z, 0.0) + jnp.log(1.0 + jnp.exp2(neg_abs)) * LOG2E
    return z, neg_log_stay, neg_log_stay.astype(BF16)


def _sb_weights(scores, tri, v_blk, r, sub):
    z, neg_log_stay, nls_bf16 = scores
    n_sub = z.shape[1] // sub
    parts = [None] * n_sub
    for j in range(n_sub - 1, -1, -1):
        cols = slice(j * sub, (j + 1) * sub)
        log_a = jnp.dot(nls_bf16[:, cols], tri, preferred_element_type=F32) + z[:, cols]
        tot = jnp.broadcast_to(jnp.sum(neg_log_stay[:, cols], axis=1, keepdims=True),
                               (z.shape[0], V7X_LANES))
        if r is None:
            r = -tot
        else:
            log_a = log_a + jnp.concatenate([r] * (sub // V7X_LANES), axis=1)
            r = r - tot
        parts[j] = jnp.exp2(log_a).astype(BF16)
    a = parts[0] if n_sub == 1 else jnp.concatenate(parts, axis=1)
    return jnp.dot(a, v_blk, preferred_element_type=F32), r


def _sb_chunk(q, k_blk, v_blk, k_mask, tri, r_ref, acc_ref, sub):
    out, r = _sb_weights(_sb_scores(q, k_blk, k_mask), tri, v_blk, r_ref[...], sub)
    acc_ref[...] += out
    r_ref[...] = r


def _sb_kernel(q_ref, k_ref, v_ref, km_ref, vm_ref, tri_ref, o_ref, r_ref, acc_ref,
               *, tq, n_meta, heads, qblocks):
    tri = tri_ref[...]
    head_cols = [slice(hh * HEAD_DIM, (hh + 1) * HEAD_DIM) for hh in range(heads)]
    col_minus_row = (lax.broadcasted_iota(jnp.int32, (tq, 2 * tq), 1)
                     - lax.broadcasted_iota(jnp.int32, (tq, 2 * tq), 0))
    rows = [slice(qb * tq, (qb + 1) * tq) for qb in range(qblocks)]
    qi = [pl.program_id(2) * qblocks + qb for qb in range(qblocks)]
    first = [jnp.maximum(q - 1, 0) for q in qi]

    scores = {}
    for qb in range(qblocks):
        start = pl.multiple_of(first[qb] * tq, tq)
        causal = col_minus_row < (qi[qb] - first[qb]) * tq
        for hh, cols in enumerate(head_cols):
            scores[qb, hh] = _sb_scores(q_ref[rows[qb], cols],
                                        k_ref[pl.ds(start, 2 * tq), cols], causal)
    for qb in range(qblocks):
        start = pl.multiple_of(first[qb] * tq, tq)
        for hh, cols in enumerate(head_cols):
            out, r = _sb_weights(scores[qb, hh], tri, v_ref[pl.ds(start, 2 * tq), cols],
                                 None, tq)
            acc_ref[qb, hh] = out
            r_ref[qb, hh] = r

    for qb in range(qblocks):
        def more_keys_matter(state, qb=qb):
            n_done, r_max = state
            return jnp.logical_and(n_done < first[qb], r_max > PRUNE_LOG2)

        def earlier_chunk(state, qb=qb):
            n_done, _ = state
            start = pl.multiple_of((first[qb] - 1 - n_done) * tq, tq)
            for hh, cols in enumerate(head_cols):
                _sb_chunk(q_ref[rows[qb], cols], k_ref[pl.ds(start, tq), cols],
                          v_ref[pl.ds(start, tq), cols], None, tri,
                          r_ref.at[qb, hh], acc_ref.at[qb, hh], tq)
            return n_done + 1, jnp.max(r_ref[qb])

        _, r_max = lax.while_loop(more_keys_matter, earlier_chunk,
                                  (jnp.int32(0), jnp.max(r_ref[qb])))

        @pl.when(r_max > PRUNE_LOG2)
        def _(qb=qb):
            n_pad = km_ref.shape[0]
            is_meta = lax.broadcasted_iota(jnp.int32, (tq, n_pad), 1) < n_meta
            for hh, cols in enumerate(head_cols):
                _sb_chunk(q_ref[rows[qb], cols], km_ref[:, cols], vm_ref[:, cols], is_meta,
                          tri_ref[0:n_pad, 0:n_pad], r_ref.at[qb, hh], acc_ref.at[qb, hh],
                          n_pad)

        for hh, cols in enumerate(head_cols):
            o_ref[rows[qb], cols] = acc_ref[qb, hh].astype(o_ref.dtype)


def _sb_attention(qkv, meta_kv, *, n_batch, seq, n_heads, n_meta, tq, heads, qblocks):
    m = qkv.shape[0]
    d = n_heads * HEAD_DIM
    n_pad = meta_kv.shape[0]
    tstep = qblocks * tq
    nq = seq // tstep
    assert seq % tstep == 0 and seq >= 2 * tq and n_pad <= tq and n_heads % heads == 0
    groups = n_heads // heads
    width = heads * HEAD_DIM
    tri = jnp.where(lax.broadcasted_iota(jnp.int32, (tq, tq), 0)
                    >= lax.broadcasted_iota(jnp.int32, (tq, tq), 1), -1.0, 0.0).astype(BF16)
    pairs = qblocks * heads
    est = (2 * 2 * tstep * width * 2 + 2 * 2 * seq * width * 2
           + 2 * 2 * n_pad * width * 2 + 2 * tq * tq * 2
           + pairs * tq * (V7X_LANES + HEAD_DIM) * 4 + pairs * 4 * tq * 2 * tq * 4)
    kern = functools.partial(_sb_kernel, tq=tq, n_meta=n_meta, heads=heads, qblocks=qblocks)
    return pl.pallas_call(
        kern,
        out_shape=jax.ShapeDtypeStruct((m, d), BF16),
        grid=(n_batch, groups, nq),
        in_specs=[
            pl.BlockSpec((tstep, width), lambda b, g, i: (b * nq + i, g)),
            pl.BlockSpec((seq, width), lambda b, g, i: (b, groups + g)),
            pl.BlockSpec((seq, width), lambda b, g, i: (b, 2 * groups + g)),
            pl.BlockSpec((n_pad, width), lambda b, g, i: (0, groups + g)),
            pl.BlockSpec((n_pad, width), lambda b, g, i: (0, 2 * groups + g)),
            pl.BlockSpec((tq, tq), lambda b, g, i: (0, 0)),
        ],
        out_specs=pl.BlockSpec((tstep, width), lambda b, g, i: (b * nq + i, g)),
        scratch_shapes=[pltpu.VMEM((qblocks, heads, tq, V7X_LANES), F32),
                        pltpu.VMEM((qblocks, heads, tq, HEAD_DIM), F32)],
        compiler_params=pltpu.CompilerParams(
            dimension_semantics=("parallel", "parallel", "arbitrary"),
            vmem_limit_bytes=_vmem_limit(est)),
        name="stick_breaking_attention",
    )(qkv, qkv, qkv, meta_kv, meta_kv, tri)


def _oproj_kernel(x_ref, o_ref, w_ref, out_ref):
    out_ref[...] = x_ref[...] + jnp.dot(o_ref[...], w_ref[...], preferred_element_type=F32)


def _out_proj(x, o, w_o, *, tm, tn):
    m, d = x.shape
    est = 2 * (tm * tn * 4 * 2 + tm * d * 2 + d * tn * 2)
    return pl.pallas_call(
        _oproj_kernel,
        out_shape=jax.ShapeDtypeStruct((m, d), F32),
        grid=(m // tm, d // tn),
        in_specs=[
            pl.BlockSpec((tm, tn), lambda i, j: (i, j)),
            pl.BlockSpec((tm, d), lambda i, j: (i, 0)),
            pl.BlockSpec((d, tn), lambda i, j: (0, j)),
        ],
        out_specs=pl.BlockSpec((tm, tn), lambda i, j: (i, j)),
        compiler_params=pltpu.CompilerParams(
            dimension_semantics=("parallel", "arbitrary"),
            vmem_limit_bytes=_vmem_limit(est)),
        name="attn_out_proj",
    )(x, o, w_o)


def kernel(x, meta, ffn_norm, ffn_w_in, ffn_w_out, mix_norm, pool_w, pool_scale,
           sb_w_qkv, sb_qk_norm, sb_w_o):
    n_batch, seq, d = x.shape
    n_meta = meta.shape[0]
    d_ff = ffn_w_out.shape[2]
    n_heads = d // HEAD_DIM
    assert ffn_norm.shape[0] == 2 and pool_w.shape[0] == 1 and sb_w_qkv.shape[0] == 1
    assert n_meta == MAX_WINDOW
    assert (n_batch * seq) % FFN_ROW_TILE == 0 and seq % ROW_TILE == 0
    assert d_ff % FFN_FF_TILE == 0 and d % QKV_COL_TILE == 0 and d % OUT_PROJ_COL_TILE == 0

    w00 = (ffn_w_in[0, 0].astype(BF16), (0.5 * ffn_w_out[0, 0]).astype(BF16))
    w_pool = pool_w[0].astype(BF16)
    w_qkv = sb_w_qkv[0].astype(BF16)
    w_o = sb_w_o[0].astype(BF16)

    def ffn(h, layer, half, weights, tm, next_sublayer=None):
        nxt = None if next_sublayer is None else (ffn_w_in, ffn_w_out) + next_sublayer
        return _ffn_half(h, ffn_norm[layer, half], weights[0], weights[1], nxt,
                         tm=tm, tf=FFN_FF_TILE)

    h = x.reshape(n_batch * seq, d)
    h, *w01 = ffn(h, 0, 0, w00, FFN_ROW_TILE, next_sublayer=(0, 1))

    hm = ffn(meta.astype(F32), 0, 0, w00, n_meta)
    hm_pre_pool = hm
    hm = _pool_mixer(hm, jnp.zeros_like(hm), mix_norm[0], w_pool, pool_scale[0],
                     tm=n_meta, seq=n_meta, pos0=0)

    h = _pool_mixer(h, hm_pre_pool, mix_norm[0], w_pool, pool_scale[0],
                    tm=ROW_TILE, seq=seq, pos0=n_meta)
    h, *w10 = ffn(h, 0, 1, w01, FFN_ROW_TILE, next_sublayer=(1, 0))
    h, *w11 = ffn(h, 1, 0, w10, FFN_ROW_TILE, next_sublayer=(1, 1))

    hm = ffn(hm, 0, 1, w01, n_meta)
    hm = ffn(hm, 1, 0, w10, n_meta)
    meta_qkv = _qkv_proj(hm, mix_norm[1], w_qkv, sb_qk_norm[0],
                         tm=n_meta, tn=META_QKV_COL_TILE)
    meta_kv = jnp.pad(meta_qkv, ((0, V7X_LANES - n_meta), (0, 0)))

    qkv = _qkv_proj(h, mix_norm[1], w_qkv, sb_qk_norm[0], tm=ROW_TILE, tn=QKV_COL_TILE)
    o = _sb_attention(qkv, meta_kv, n_batch=n_batch, seq=seq, n_heads=n_heads,
                      n_meta=n_meta, tq=ATTN_Q_ROWS, heads=ATTN_HEADS_PER_STEP,
                      qblocks=ATTN_QBLOCKS_PER_STEP)
    h = _out_proj(h, o, w_o, tm=ROW_TILE, tn=OUT_PROJ_COL_TILE)
    h = ffn(h, 1, 1, w11, FFN_ROW_TILE)
    return h.reshape(n_batch, seq, d)
```

```python
import functools
import math

import jax
import jax.numpy as jnp
from jax import lax
from jax.experimental import pallas as pl
from jax.experimental.pallas import tpu as pltpu

RMS_EPS = 1e-6
POOL_WINDOWS = (2, 4, 8, 16)
MAX_WINDOW = max(POOL_WINDOWS)
HEAD_DIM = 128
LOG2E = math.log2(math.e)

V7X_LANES = 128
V7X_VMEM_BYTES = 64 * 1024 * 1024

F32 = jnp.float32
BF16 = jnp.bfloat16

FFN_ROW_TILE = 1024
FFN_FF_TILE = 512
ROW_TILE = 512
QKV_COL_TILE = 2048
META_QKV_COL_TILE = 512
OUT_PROJ_COL_TILE = 2048
ATTN_Q_ROWS = 256
ATTN_HEADS_PER_STEP = 4
ATTN_QBLOCKS_PER_STEP = 2

VMEM_ESTIMATE_MARGIN = 1.25
VMEM_TEMPORARIES_BYTES = 4 * 1024 * 1024
VMEM_MAX_FRACTION = 0.9


def _vmem_limit(estimate_bytes):
    return int(min(estimate_bytes * VMEM_ESTIMATE_MARGIN + VMEM_TEMPORARIES_BYTES,
                   V7X_VMEM_BYTES * VMEM_MAX_FRACTION))


def _rms_norm_rows(x, gain):
    ms = jnp.mean(x * x, axis=-1, keepdims=True)
    return x * lax.rsqrt(ms + RMS_EPS) * gain


def _ffn_kernel(*refs, cast_next):
    if cast_next:
        (x_ref, g_ref, wg_ref, wu_ref, wo_ref, next_wi_ref, next_wo_ref,
         o_ref, next_wi_bf_ref, next_wo_bf_ref, xn_ref) = refs
    else:
        x_ref, g_ref, wg_ref, wu_ref, wo_ref, o_ref, xn_ref = refs
    j = pl.program_id(1)

    def ff_tile(first):
        xn = xn_ref[...]
        gate = jnp.dot(xn, wg_ref[...], preferred_element_type=F32)
        up = jnp.dot(xn, wu_ref[...], preferred_element_type=F32)
        act = (gate * jax.nn.sigmoid(gate) * up).astype(BF16)
        contrib = jnp.dot(act, wo_ref[...], preferred_element_type=F32)
        if first:
            o_ref[...] = x_ref[...] + contrib
        else:
            o_ref[...] += contrib
        if cast_next:
            next_wi_bf_ref[...] = next_wi_ref[...].astype(BF16)
            next_wo_bf_ref[...] = (0.5 * next_wo_ref[...]).astype(BF16)

    @pl.when(j == 0)
    def _():
        xn_ref[...] = _rms_norm_rows(x_ref[...], g_ref[...]).astype(BF16)
        ff_tile(True)

    @pl.when(j > 0)
    def _():
        ff_tile(False)


def _ffn_half(x, gain, w_in, w_out, next_weights=None, *, tm, tf):
    m, d = x.shape
    d_ff = w_out.shape[0]
    n_ff = d_ff // tf
    n_rows = m // tm
    est = (2 * 2 * tm * d * 4 + tm * d * 2 + 2 * (2 * d * tf + tf * d) * 2
           + 3 * tm * tf * 4)
    in_specs = [
        pl.BlockSpec((tm, d), lambda i, j: (i, 0)),
        pl.BlockSpec((1, d), lambda i, j: (0, 0)),
        pl.BlockSpec((d, tf), lambda i, j: (0, j)),
        pl.BlockSpec((d, tf), lambda i, j: (0, n_ff + j)),
        pl.BlockSpec((tf, d), lambda i, j: (j, 0)),
    ]
    operands = [x, gain.reshape(1, d), w_in, w_in, w_out]
    out_shape = [jax.ShapeDtypeStruct((m, d), F32)]
    out_specs = [pl.BlockSpec((tm, d), lambda i, j: (i, 0))]
    if next_weights is not None:
        next_wi, next_wo, layer, half = next_weights
        wi_rows, wi_cols = d // n_rows, 2 * d_ff // n_ff
        wo_rows = d_ff // (n_rows * n_ff)
        assert wi_rows * n_rows == d and wi_cols * n_ff == 2 * d_ff
        assert wo_rows * n_rows * n_ff == d_ff
        assert wi_rows % 16 == 0 and wo_rows % 16 == 0 and wi_cols % V7X_LANES == 0
        in_specs += [
            pl.BlockSpec((None, None, wi_rows, wi_cols), lambda i, j: (layer, half, i, j)),
            pl.BlockSpec((None, None, wo_rows, d), lambda i, j: (layer, half, i * n_ff + j, 0)),
        ]
        operands += [next_wi, next_wo]
        out_shape += [jax.ShapeDtypeStruct((d, 2 * d_ff), BF16),
                      jax.ShapeDtypeStruct((d_ff, d), BF16)]
        out_specs += [pl.BlockSpec((wi_rows, wi_cols), lambda i, j: (i, j)),
                      pl.BlockSpec((wo_rows, d), lambda i, j: (i * n_ff + j, 0))]
        est += 2 * (wi_rows * wi_cols + wo_rows * d) * (4 + 2)
    outs = pl.pallas_call(
        functools.partial(_ffn_kernel, cast_next=next_weights is not None),
        out_shape=out_shape,
        grid=(n_rows, n_ff),
        in_specs=in_specs,
        out_specs=out_specs,
        scratch_shapes=[pltpu.VMEM((tm, d), BF16)],
        compiler_params=pltpu.CompilerParams(
            dimension_semantics=("parallel", "arbitrary"),
            vmem_limit_bytes=_vmem_limit(est)),
        name="ffn_half",
    )(*operands)
    return outs[0] if next_weights is None else tuple(outs)


def _pool_kernel(x_ref, halo_ref, first_halo_ref, g_ref, w_ref, s_ref, o_ref,
                 ext_ref, *, tm, tiles_per_seq, pos0):
    i = pl.program_id(0)
    t = i % tiles_per_seq
    gain = g_ref[...]
    x = x_ref[...]
    halo = jnp.where(t == 0, first_halo_ref[...], halo_ref[...])
    ext_ref[0:MAX_WINDOW, :] = _rms_norm_rows(halo, gain)
    ext_ref[MAX_WINDOW:, :] = _rms_norm_rows(x, gain)
    pos = pos0 + t * tm + lax.broadcasted_iota(jnp.int32, (tm, 1), 0)
    group = x.shape[1] // len(POOL_WINDOWS)
    for gi, w in enumerate(POOL_WINDOWS):
        assert w & (w - 1) == 0 and w <= MAX_WINDOW
        cols = slice(gi * group, (gi + 1) * group)
        tot = ext_ref[:, cols]
        step = 1
        while step < w:
            tot = tot + pltpu.roll(tot, step, axis=0)
            step *= 2
        tot = tot[MAX_WINDOW:]
        hn = ext_ref[MAX_WINDOW:, cols]
        cnt = jnp.minimum(pos + 1, w).astype(F32)
        y = (tot / cnt - hn).astype(BF16)
        mixed = jnp.dot(y, w_ref[gi], preferred_element_type=F32)
        o_ref[:, cols] = x[:, cols] + s_ref[:, cols] * mixed


def _pool_mixer(x, first_halo, gain, w_pool, scale, *, tm, seq, pos0):
    m, d = x.shape
    tiles_per_seq = seq // tm
    halo_blocks = tm // MAX_WINDOW
    est = 2 * 2 * tm * d * 4 + (tm + MAX_WINDOW) * d * 4 + 2 * w_pool.size * 2
    kern = functools.partial(_pool_kernel, tm=tm, tiles_per_seq=tiles_per_seq, pos0=pos0)
    return pl.pallas_call(
        kern,
        out_shape=jax.ShapeDtypeStruct((m, d), F32),
        grid=(m // tm,),
        in_specs=[
            pl.BlockSpec((tm, d), lambda i: (i, 0)),
            pl.BlockSpec((MAX_WINDOW, d), lambda i: (jnp.maximum(i * halo_blocks - 1, 0), 0)),
            pl.BlockSpec((MAX_WINDOW, d), lambda i: (0, 0)),
            pl.BlockSpec((1, d), lambda i: (0, 0)),
            pl.BlockSpec(w_pool.shape, lambda i: (0, 0, 0)),
            pl.BlockSpec((1, d), lambda i: (0, 0)),
        ],
        out_specs=pl.BlockSpec((tm, d), lambda i: (i, 0)),
        scratch_shapes=[pltpu.VMEM((tm + MAX_WINDOW, d), F32)],
        compiler_params=pltpu.CompilerParams(
            dimension_semantics=("arbitrary",),
            vmem_limit_bytes=_vmem_limit(est)),
        name="pool_mixer",
    )(x, x, first_halo, gain.reshape(1, d), w_pool, scale.reshape(1, d))


def _qkv_kernel(x_ref, g_ref, w_ref, cg_ref, o_ref, xn_ref, *, tn, d):
    j = pl.program_id(1)

    @pl.when(j == 0)
    def _():
        xn_ref[...] = _rms_norm_rows(x_ref[...], g_ref[...]).astype(BF16)

    res = jnp.dot(xn_ref[...], w_ref[...], preferred_element_type=F32)
    is_v = j >= 2 * (d // tn)
    for hh in range(tn // HEAD_DIM):
        cols = slice(hh * HEAD_DIM, (hh + 1) * HEAD_DIM)
        blk = res[:, cols]
        ms = jnp.mean(blk * blk, axis=-1, keepdims=True)
        inv = jnp.where(is_v, 1.0, lax.rsqrt(ms + RMS_EPS))
        o_ref[:, cols] = (blk * inv * cg_ref[:, cols]).astype(o_ref.dtype)


def _qkv_proj(x, gain, w_qkv, qk_norm, *, tm, tn):
    m, d = x.shape
    n = w_qkv.shape[1]
    n_heads = d // HEAD_DIM
    col_gain = jnp.concatenate([
        jnp.tile(qk_norm[0].astype(F32) * (HEAD_DIM ** -0.5 * LOG2E), n_heads),
        jnp.tile(qk_norm[1].astype(F32), n_heads),
        jnp.ones((d,), F32)]).reshape(1, n)
    est = 2 * tm * d * 4 + tm * d * 2 + 2 * d * tn * 2 + 2 * tm * tn * 2 + tm * tn * 4
    kern = functools.partial(_qkv_kernel, tn=tn, d=d)
    return pl.pallas_call(
        kern,
        out_shape=jax.ShapeDtypeStruct((m, n), BF16),
        grid=(m // tm, n // tn),
        in_specs=[
            pl.BlockSpec((tm, d), lambda i, j: (i, 0)),
            pl.BlockSpec((1, d), lambda i, j: (0, 0)),
            pl.BlockSpec((d, tn), lambda i, j: (0, j)),
            pl.BlockSpec((1, tn), lambda i, j: (0, j)),
        ],
        out_specs=pl.BlockSpec((tm, tn), lambda i, j: (i, j)),
        scratch_shapes=[pltpu.VMEM((tm, d), BF16)],
        compiler_params=pltpu.CompilerParams(
            dimension_semantics=("parallel", "arbitrary"),
            vmem_limit_bytes=_vmem_limit(est)),
        name="qkv_proj",
    )(x, gain.reshape(1, d), w_qkv, col_gain)


MASKED_LOGIT = -1e30
PRUNE_LOG2 = -150.0


def _sb_scores(q, k_blk, k_mask):
    z = lax.dot_general(q, k_blk, (((1,), (1,)), ((), ())), preferred_element_type=F32)
    if k_mask is not None:
        z = jnp.where(k_mask, z, MASKED_LOGIT)
    neg_abs = pltpu.bitcast(pltpu.bitcast(z, jnp.uint32) | jnp.uint32(0x80000000), F32)
    neg_log_stay = jnp.maximum(z, 0.0) + jnp.log(1.0 + jnp.exp2(neg_abs)) * LOG2E
    return z, neg_log_stay, neg_log_stay.astype(BF16)


def _sb_weights(scores, tri, v_blk, r, sub):
    z, neg_log_stay, nls_bf16 = scores
    n_sub = z.shape[1] // sub
    parts = [None] * n_sub
    for j in range(n_sub - 1, -1, -1):
        cols = slice(j * sub, (j + 1) * sub)
        log_a = jnp.dot(nls_bf16[:, cols], tri, preferred_element_type=F32) + z[:, cols]
        tot = jnp.broadcast_to(jnp.sum(neg_log_stay[:, cols], axis=1, keepdims=True),
                               (z.shape[0], V7X_LANES))
        if r is None:
            r = -tot
        else:
            log_a = log_a + jnp.concatenate([r] * (sub // V7X_LANES), axis=1)
            r = r - tot
        parts[j] = jnp.exp2(log_a).astype(BF16)
    a = parts[0] if n_sub == 1 else jnp.concatenate(parts, axis=1)
    return jnp.dot(a, v_blk, preferred_element_type=F32), r


def _sb_chunk(q, k_blk, v_blk, k_mask, tri, r_ref, acc_ref, sub):
    out, r = _sb_weights(_sb_scores(q, k_blk, k_mask), tri, v_blk, r_ref[...], sub)
    acc_ref[...] += out
    r_ref[...] = r


def _sb_kernel(q_ref, k_ref, v_ref, km_ref, vm_ref, tri_ref, o_ref, r_ref, acc_ref,
               *, tq, n_meta, heads, qblocks):
    tri = tri_ref[...]
    head_cols = [slice(hh * HEAD_DIM, (hh + 1) * HEAD_DIM) for hh in range(heads)]
    col_minus_row = (lax.broadcasted_iota(jnp.int32, (tq, 2 * tq), 1)
                     - lax.broadcasted_iota(jnp.int32, (tq, 2 * tq), 0))
    rows = [slice(qb * tq, (qb + 1) * tq) for qb in range(qblocks)]
    qi = [pl.program_id(2) * qblocks + qb for qb in range(qblocks)]
    first = [jnp.maximum(q - 1, 0) for q in qi]

    scores = {}
    for qb in range(qblocks):
        start = pl.multiple_of(first[qb] * tq, tq)
        causal = col_minus_row < (qi[qb] - first[qb]) * tq
        for hh, cols in enumerate(head_cols):
            scores[qb, hh] = _sb_scores(q_ref[rows[qb], cols],
                                        k_ref[pl.ds(start, 2 * tq), cols], causal)
    for qb in range(qblocks):
        start = pl.multiple_of(first[qb] * tq, tq)
        for hh, cols in enumerate(head_cols):
            out, r = _sb_weights(scores[qb, hh], tri, v_ref[pl.ds(start, 2 * tq), cols],
                                 None, tq)
            acc_ref[qb, hh] = out
            r_ref[qb, hh] = r

    for qb in range(qblocks):
        def more_keys_matter(state, qb=qb):
            n_done, r_max = state
            return jnp.logical_and(n_done < first[qb], r_max > PRUNE_LOG2)

        def earlier_chunk(state, qb=qb):
            n_done, _ = state
            start = pl.multiple_of((first[qb] - 1 - n_done) * tq, tq)
            for hh, cols in enumerate(head_cols):
                _sb_chunk(q_ref[rows[qb], cols], k_ref[pl.ds(start, tq), cols],
                          v_ref[pl.ds(start, tq), cols], None, tri,
                          r_ref.at[qb, hh], acc_ref.at[qb, hh], tq)
            return n_done + 1, jnp.max(r_ref[qb])

        _, r_max = lax.while_loop(more_keys_matter, earlier_chunk,
                                  (jnp.int32(0), jnp.max(r_ref[qb])))

        @pl.when(r_max > PRUNE_LOG2)
        def _(qb=qb):
            n_pad = km_ref.shape[0]
            is_meta = lax.broadcasted_iota(jnp.int32, (tq, n_pad), 1) < n_meta
            for hh, cols in enumerate(head_cols):
                _sb_chunk(q_ref[rows[qb], cols], km_ref[:, cols], vm_ref[:, cols], is_meta,
                          tri_ref[0:n_pad, 0:n_pad], r_ref.at[qb, hh], acc_ref.at[qb, hh],
                          n_pad)

        for hh, cols in enumerate(head_cols):
            o_ref[rows[qb], cols] = acc_ref[qb, hh].astype(o_ref.dtype)


def _sb_attention(qkv, meta_kv, *, n_batch, seq, n_heads, n_meta, tq, heads, qblocks):
    m = qkv.shape[0]
    d = n_heads * HEAD_DIM
    n_pad = meta_kv.shape[0]
    tstep = qblocks * tq
    nq = seq // tstep
    assert seq % tstep == 0 and seq >= 2 * tq and n_pad <= tq and n_heads % heads == 0
    groups = n_heads // heads
    width = heads * HEAD_DIM
    tri = jnp.where(lax.broadcasted_iota(jnp.int32, (tq, tq), 0)
                    >= lax.broadcasted_iota(jnp.int32, (tq, tq), 1), -1.0, 0.0).astype(BF16)
    pairs = qblocks * heads
    est = (2 * 2 * tstep * width * 2 + 2 * 2 * seq * width * 2
           + 2 * 2 * n_pad * width * 2 + 2 * tq * tq * 2
           + pairs * tq * (V7X_LANES + HEAD_DIM) * 4 + pairs * 4 * tq * 2 * tq * 4)
    kern = functools.partial(_sb_kernel, tq=tq, n_meta=n_meta, heads=heads, qblocks=qblocks)
    return pl.pallas_call(
        kern,
        out_shape=jax.ShapeDtypeStruct((m, d), BF16),
        grid=(n_batch, groups, nq),
        in_specs=[
            pl.BlockSpec((tstep, width), lambda b, g, i: (b * nq + i, g)),
            pl.BlockSpec((seq, width), lambda b, g, i: (b, groups + g)),
            pl.BlockSpec((seq, width), lambda b, g, i: (b, 2 * groups + g)),
            pl.BlockSpec((n_pad, width), lambda b, g, i: (0, groups + g)),
            pl.BlockSpec((n_pad, width), lambda b, g, i: (0, 2 * groups + g)),
            pl.BlockSpec((tq, tq), lambda b, g, i: (0, 0)),
        ],
        out_specs=pl.BlockSpec((tstep, width), lambda b, g, i: (b * nq + i, g)),
        scratch_shapes=[pltpu.VMEM((qblocks, heads, tq, V7X_LANES), F32),
                        pltpu.VMEM((qblocks, heads, tq, HEAD_DIM), F32)],
        compiler_params=pltpu.CompilerParams(
            dimension_semantics=("parallel", "parallel", "arbitrary"),
            vmem_limit_bytes=_vmem_limit(est)),
        name="stick_breaking_attention",
    )(qkv, qkv, qkv, meta_kv, meta_kv, tri)


def _oproj_kernel(x_ref, o_ref, w_ref, out_ref):
    out_ref[...] = x_ref[...] + jnp.dot(o_ref[...], w_ref[...], preferred_element_type=F32)


def _out_proj(x, o, w_o, *, tm, tn):
    m, d = x.shape
    est = 2 * (tm * tn * 4 * 2 + tm * d * 2 + d * tn * 2)
    return pl.pallas_call(
        _oproj_kernel,
        out_shape=jax.ShapeDtypeStruct((m, d), F32),
        grid=(m // tm, d // tn),
        in_specs=[
            pl.BlockSpec((tm, tn), lambda i, j: (i, j)),
            pl.BlockSpec((tm, d), lambda i, j: (i, 0)),
            pl.BlockSpec((d, tn), lambda i, j: (0, j)),
        ],
        out_specs=pl.BlockSpec((tm, tn), lambda i, j: (i, j)),
        compiler_params=pltpu.CompilerParams(
            dimension_semantics=("parallel", "arbitrary"),
            vmem_limit_bytes=_vmem_limit(est)),
        name="attn_out_proj",
    )(x, o, w_o)


def kernel(x, meta, ffn_norm, ffn_w_in, ffn_w_out, mix_norm, pool_w, pool_scale,
           sb_w_qkv, sb_qk_norm, sb_w_o):
    n_batch, seq, d = x.shape
    n_meta = meta.shape[0]
    d_ff = ffn_w_out.shape[2]
    n_heads = d // HEAD_DIM
    assert ffn_norm.shape[0] == 2 and pool_w.shape[0] == 1 and sb_w_qkv.shape[0] == 1
    assert n_meta == MAX_WINDOW
    assert (n_batch * seq) % FFN_ROW_TILE == 0 and seq % ROW_TILE == 0
    assert d_ff % FFN_FF_TILE == 0 and d % QKV_COL_TILE == 0 and d % OUT_PROJ_COL_TILE == 0

    w00 = (ffn_w_in[0, 0].astype(BF16), (0.5 * ffn_w_out[0, 0]).astype(BF16))
    w_pool = pool_w[0].astype(BF16)
    w_qkv = sb_w_qkv[0].astype(BF16)
    w_o = sb_w_o[0].astype(BF16)

    def ffn(h, layer, half, weights, tm, next_sublayer=None):
        nxt = None if next_sublayer is None else (ffn_w_in, ffn_w_out) + next_sublayer
        return _ffn_half(h, ffn_norm[layer, half], weights[0], weights[1], nxt,
                         tm=tm, tf=FFN_FF_TILE)

    h = x.reshape(n_batch * seq, d)
    h, *w01 = ffn(h, 0, 0, w00, FFN_ROW_TILE, next_sublayer=(0, 1))

    hm = ffn(meta.astype(F32), 0, 0, w00, n_meta)
    hm_pre_pool = hm
    hm = _pool_mixer(hm, jnp.zeros_like(hm), mix_norm[0], w_pool, pool_scale[0],
                     tm=n_meta, seq=n_meta, pos0=0)

    h = _pool_mixer(h, hm_pre_pool, mix_norm[0], w_pool, pool_scale[0],
                    tm=ROW_TILE, seq=seq, pos0=n_meta)
    h, *w10 = ffn(h, 0, 1, w01, FFN_ROW_TILE, next_sublayer=(1, 0))
    h, *w11 = ffn(h, 1, 0, w10, FFN_ROW_TILE, next_sublayer=(1, 1))

    hm = ffn(hm, 0, 1, w01, n_meta)
    hm = ffn(hm, 1, 0, w10, n_meta)
    meta_qkv = _qkv_proj(hm, mix_norm[1], w_qkv, sb_qk_norm[0],
                         tm=n_meta, tn=META_QKV_COL_TILE)
    meta_kv = jnp.pad(meta_qkv, ((0, V7X_LANES - n_meta), (0, 0)))

    qkv = _qkv_proj(h, mix_norm[1], w_qkv, sb_qk_norm[0], tm=ROW_TILE, tn=QKV_COL_TILE)
    o = _sb_attention(qkv, meta_kv, n_batch=n_batch, seq=seq, n_heads=n_heads,
                      n_meta=n_meta, tq=ATTN_Q_ROWS, heads=ATTN_HEADS_PER_STEP,
                      qblocks=ATTN_QBLOCKS_PER_STEP)
    h = _out_proj(h, o, w_o, tm=ROW_TILE, tn=OUT_PROJ_COL_TILE)
    h = ffn(h, 1, 1, w11, FFN_ROW_TILE)
    return h.reshape(n_batch, seq, d)
```

```python
import functools
import math

import jax
import jax.numpy as jnp
from jax import lax
from jax.experimental import pallas as pl
from jax.experimental.pallas import tpu as pltpu

RMS_EPS = 1e-6
POOL_WINDOWS = (2, 4, 8, 16)
MAX_WINDOW = max(POOL_WINDOWS)
HEAD_DIM = 128
LOG2E = math.log2(math.e)

V7X_LANES = 128
V7X_VMEM_BYTES = 64 * 1024 * 1024

F32 = jnp.float32
BF16 = jnp.bfloat16

FFN_ROW_TILE = 1024
FFN_FF_TILE = 512
ROW_TILE = 512
QKV_COL_TILE = 2048
META_QKV_COL_TILE = 512
OUT_PROJ_COL_TILE = 2048
ATTN_Q_ROWS = 256
ATTN_HEADS_PER_STEP = 4
ATTN_QBLOCKS_PER_STEP = 2

VMEM_ESTIMATE_MARGIN = 1.25
VMEM_TEMPORARIES_BYTES = 4 * 1024 * 1024
VMEM_MAX_FRACTION = 0.9


def _vmem_limit(estimate_bytes):
    return int(min(estimate_bytes * VMEM_ESTIMATE_MARGIN + VMEM_TEMPORARIES_BYTES,
                   V7X_VMEM_BYTES * VMEM_MAX_FRACTION))


def _rms_norm_rows(x, gain):
    ms = jnp.mean(x * x, axis=-1, keepdims=True)
    return x * lax.rsqrt(ms + RMS_EPS) * gain


def _ffn_kernel(*refs, cast_next):
    if cast_next:
        (x_ref, g_ref, wg_ref, wu_ref, wo_ref, next_wi_ref, next_wo_ref,
         o_ref, next_wi_bf_ref, next_wo_bf_ref, xn_ref) = refs
    else:
        x_ref, g_ref, wg_ref, wu_ref, wo_ref, o_ref, xn_ref = refs
    j = pl.program_id(1)

    def ff_tile(first):
        xn = xn_ref[...]
        gate = jnp.dot(xn, wg_ref[...], preferred_element_type=F32)
        up = jnp.dot(xn, wu_ref[...], preferred_element_type=F32)
        act = (gate * jax.nn.sigmoid(gate) * up).astype(BF16)
        contrib = jnp.dot(act, wo_ref[...], preferred_element_type=F32)
        if first:
            o_ref[...] = x_ref[...] + contrib
        else:
            o_ref[...] += contrib
        if cast_next:
            next_wi_bf_ref[...] = next_wi_ref[...].astype(BF16)
            next_wo_bf_ref[...] = (0.5 * next_wo_ref[...]).astype(BF16)

    @pl.when(j == 0)
    def _():
        xn_ref[...] = _rms_norm_rows(x_ref[...], g_ref[...]).astype(BF16)
        ff_tile(True)

    @pl.when(j > 0)
    def _():
        ff_tile(False)


def _ffn_half(x, gain, w_in, w_out, next_weights=None, *, tm, tf):
    m, d = x.shape
    d_ff = w_out.shape[0]
    n_ff = d_ff // tf
    n_rows = m // tm
    est = (2 * 2 * tm * d * 4 + tm * d * 2 + 2 * (2 * d * tf + tf * d) * 2
           + 3 * tm * tf * 4)
    in_specs = [
        pl.BlockSpec((tm, d), lambda i, j: (i, 0)),
        pl.BlockSpec((1, d), lambda i, j: (0, 0)),
        pl.BlockSpec((d, tf), lambda i, j: (0, j)),
        pl.BlockSpec((d, tf), lambda i, j: (0, n_ff + j)),
        pl.BlockSpec((tf, d), lambda i, j: (j, 0)),
    ]
    operands = [x, gain.reshape(1, d), w_in, w_in, w_out]
    out_shape = [jax.ShapeDtypeStruct((m, d), F32)]
    out_specs = [pl.BlockSpec((tm, d), lambda i, j: (i, 0))]
    if next_weights is not None:
        next_wi, next_wo, layer, half = next_weights
        wi_rows, wi_cols = d // n_rows, 2 * d_ff // n_ff
        wo_rows = d_ff // (n_rows * n_ff)
        assert wi_rows * n_rows == d and wi_cols * n_ff == 2 * d_ff
        assert wo_rows * n_rows * n_ff == d_ff
        assert wi_rows % 16 == 0 and wo_rows % 16 == 0 and wi_cols % V7X_LANES == 0
        in_specs += [
            pl.BlockSpec((None, None, wi_rows, wi_cols), lambda i, j: (layer, half, i, j)),
            pl.BlockSpec((None, None, wo_rows, d), lambda i, j: (layer, half, i * n_ff + j, 0)),
        ]
        operands += [next_wi, next_wo]
        out_shape += [jax.ShapeDtypeStruct((d, 2 * d_ff), BF16),
                      jax.ShapeDtypeStruct((d_ff, d), BF16)]
        out_specs += [pl.BlockSpec((wi_rows, wi_cols), lambda i, j: (i, j)),
                      pl.BlockSpec((wo_rows, d), lambda i, j: (i * n_ff + j, 0))]
        est += 2 * (wi_rows * wi_cols + wo_rows * d) * (4 + 2)
    outs = pl.pallas_call(
        functools.partial(_ffn_kernel, cast_next=next_weights is not None),
        out_shape=out_shape,
        grid=(n_rows, n_ff),
        in_specs=in_specs,
        out_specs=out_specs,
        scratch_shapes=[pltpu.VMEM((tm, d), BF16)],
        compiler_params=pltpu.CompilerParams(
            dimension_semantics=("parallel", "arbitrary"),
            vmem_limit_bytes=_vmem_limit(est)),
        name="ffn_half",
    )(*operands)
    return outs[0] if next_weights is None else tuple(outs)


def _pool_kernel(x_ref, halo_ref, first_halo_ref, g_ref, w_ref, s_ref, o_ref,
                 ext_ref, *, tm, tiles_per_seq, pos0):
    i = pl.program_id(0)
    t = i % tiles_per_seq
    gain = g_ref[...]
    x = x_ref[...]
    halo = jnp.where(t == 0, first_halo_ref[...], halo_ref[...])
    ext_ref[0:MAX_WINDOW, :] = _rms_norm_rows(halo, gain)
    ext_ref[MAX_WINDOW:, :] = _rms_norm_rows(x, gain)
    pos = pos0 + t * tm + lax.broadcasted_iota(jnp.int32, (tm, 1), 0)
    group = x.shape[1] // len(POOL_WINDOWS)
    for gi, w in enumerate(POOL_WINDOWS):
        assert w & (w - 1) == 0 and w <= MAX_WINDOW
        cols = slice(gi * group, (gi + 1) * group)
        tot = ext_ref[:, cols]
        step = 1
        while step < w:
            tot = tot + pltpu.roll(tot, step, axis=0)
            step *= 2
        tot = tot[MAX_WINDOW:]
        hn = ext_ref[MAX_WINDOW:, cols]
        cnt = jnp.minimum(pos + 1, w).astype(F32)
        y = (tot / cnt - hn).astype(BF16)
        mixed = jnp.dot(y, w_ref[gi], preferred_element_type=F32)
        o_ref[:, cols] = x[:, cols] + s_ref[:, cols] * mixed


def _pool_mixer(x, first_halo, gain, w_pool, scale, *, tm, seq, pos0):
    m, d = x.shape
    tiles_per_seq = seq // tm
    halo_blocks = tm // MAX_WINDOW
    est = 2 * 2 * tm * d * 4 + (tm + MAX_WINDOW) * d * 4 + 2 * w_pool.size * 2
    kern = functools.partial(_pool_kernel, tm=tm, tiles_per_seq=tiles_per_seq, pos0=pos0)
    return pl.pallas_call(
        kern,
        out_shape=jax.ShapeDtypeStruct((m, d), F32),
        grid=(m // tm,),
        in_specs=[
            pl.BlockSpec((tm, d), lambda i: (i, 0)),
            pl.BlockSpec((MAX_WINDOW, d), lambda i: (jnp.maximum(i * halo_blocks - 1, 0), 0)),
            pl.BlockSpec((MAX_WINDOW, d), lambda i: (0, 0)),
            pl.BlockSpec((1, d), lambda i: (0, 0)),
            pl.BlockSpec(w_pool.shape, lambda i: (0, 0, 0)),
            pl.BlockSpec((1, d), lambda i: (0, 0)),
        ],
        out_specs=pl.BlockSpec((tm, d), lambda i: (i, 0)),
        scratch_shapes=[pltpu.VMEM((tm + MAX_WINDOW, d), F32)],
        compiler_params=pltpu.CompilerParams(
            dimension_semantics=("arbitrary",),
            vmem_limit_bytes=_vmem_limit(est)),
        name="pool_mixer",
    )(x, x, first_halo, gain.reshape(1, d), w_pool, scale.reshape(1, d))


def _qkv_kernel(x_ref, g_ref, w_ref, cg_ref, o_ref, xn_ref, *, tn, d):
    j = pl.program_id(1)

    def column_tile():
        res = jnp.dot(xn_ref[...], w_ref[...], preferred_element_type=F32)
        is_v = j >= 2 * (d // tn)
        for hh in range(tn // HEAD_DIM):
            cols = slice(hh * HEAD_DIM, (hh + 1) * HEAD_DIM)
            blk = res[:, cols]
            ms = jnp.mean(blk * blk, axis=-1, keepdims=True)
            inv = jnp.where(is_v, 1.0, lax.rsqrt(ms + RMS_EPS))
            o_ref[:, cols] = (blk * inv * cg_ref[:, cols]).astype(o_ref.dtype)

    @pl.when(j == 0)
    def _():
        xn_ref[...] = _rms_norm_rows(x_ref[...], g_ref[...]).astype(BF16)
        column_tile()

    @pl.when(j > 0)
    def _():
        column_tile()


def _qkv_proj(x, gain, w_qkv, qk_norm, *, tm, tn):
    m, d = x.shape
    n = w_qkv.shape[1]
    n_heads = d // HEAD_DIM
    col_gain = jnp.concatenate([
        jnp.tile(qk_norm[0].astype(F32) * (HEAD_DIM ** -0.5 * LOG2E), n_heads),
        jnp.tile(qk_norm[1].astype(F32), n_heads),
        jnp.ones((d,), F32)]).reshape(1, n)
    est = 2 * tm * d * 4 + tm * d * 2 + 2 * d * tn * 2 + 2 * tm * tn * 2 + tm * tn * 4
    kern = functools.partial(_qkv_kernel, tn=tn, d=d)
    return pl.pallas_call(
        kern,
        out_shape=jax.ShapeDtypeStruct((m, n), BF16),
        grid=(m // tm, n // tn),
        in_specs=[
            pl.BlockSpec((tm, d), lambda i, j: (i, 0)),
            pl.BlockSpec((1, d), lambda i, j: (0, 0)),
            pl.BlockSpec((d, tn), lambda i, j: (0, j)),
            pl.BlockSpec((1, tn), lambda i, j: (0, j)),
        ],
        out_specs=pl.BlockSpec((tm, tn), lambda i, j: (i, j)),
        scratch_shapes=[pltpu.VMEM((tm, d), BF16)],
        compiler_params=pltpu.CompilerParams(
            dimension_semantics=("parallel", "arbitrary"),
            vmem_limit_bytes=_vmem_limit(est)),
        name="qkv_proj",
    )(x, gain.reshape(1, d), w_qkv, col_gain)


MASKED_LOGIT = -1e30
PRUNE_LOG2 = -150.0


def _sb_scores(q, k_blk, k_mask):
    z = lax.dot_general(q, k_blk, (((1,), (1,)), ((), ())), preferred_element_type=F32)
    if k_mask is not None:
        z = jnp.where(k_mask, z, MASKED_LOGIT)
    neg_abs = pltpu.bitcast(pltpu.bitcast(z, jnp.uint32) | jnp.uint32(0x80000000), F32)
    neg_log_stay = jnp.maximum(z, 0.0) + jnp.log(1.0 + jnp.exp2(neg_abs)) * LOG2E
    return z, neg_log_stay, neg_log_stay.astype(BF16)


def _sb_weights(scores, tri, v_blk, r, sub):
    z, neg_log_stay, nls_bf16 = scores
    n_sub = z.shape[1] // sub
    parts = [None] * n_sub
    for j in range(n_sub - 1, -1, -1):
        cols = slice(j * sub, (j + 1) * sub)
        log_a = jnp.dot(nls_bf16[:, cols], tri, preferred_element_type=F32) + z[:, cols]
        tot = jnp.broadcast_to(jnp.sum(neg_log_stay[:, cols], axis=1, keepdims=True),
                               (z.shape[0], V7X_LANES))
        if r is None:
            r = -tot
        else:
            log_a = log_a + jnp.concatenate([r] * (sub // V7X_LANES), axis=1)
            r = r - tot
        parts[j] = jnp.exp2(log_a).astype(BF16)
    a = parts[0] if n_sub == 1 else jnp.concatenate(parts, axis=1)
    return jnp.dot(a, v_blk, preferred_element_type=F32), r


def _sb_chunk(q, k_blk, v_blk, k_mask, tri, r_ref, acc_ref, sub):
    out, r = _sb_weights(_sb_scores(q, k_blk, k_mask), tri, v_blk, r_ref[...], sub)
    acc_ref[...] += out
    r_ref[...] = r


def _sb_kernel(q_ref, k_ref, v_ref, km_ref, vm_ref, tri_ref, o_ref, r_ref, acc_ref,
               *, tq, n_meta, heads, qblocks):
    tri = tri_ref[...]
    head_cols = [slice(hh * HEAD_DIM, (hh + 1) * HEAD_DIM) for hh in range(heads)]
    col_minus_row = (lax.broadcasted_iota(jnp.int32, (tq, 2 * tq), 1)
                     - lax.broadcasted_iota(jnp.int32, (tq, 2 * tq), 0))
    rows = [slice(qb * tq, (qb + 1) * tq) for qb in range(qblocks)]
    qi = [pl.program_id(2) * qblocks + qb for qb in range(qblocks)]
    first = [jnp.maximum(q - 1, 0) for q in qi]

    scores = {}
    for qb in range(qblocks):
        start = pl.multiple_of(first[qb] * tq, tq)
        causal = col_minus_row < (qi[qb] - first[qb]) * tq
        for hh, cols in enumerate(head_cols):
            scores[qb, hh] = _sb_scores(q_ref[rows[qb], cols],
                                        k_ref[pl.ds(start, 2 * tq), cols], causal)
    for qb in range(qblocks):
        start = pl.multiple_of(first[qb] * tq, tq)
        for hh, cols in enumerate(head_cols):
            out, r = _sb_weights(scores[qb, hh], tri, v_ref[pl.ds(start, 2 * tq), cols],
                                 None, tq)
            acc_ref[qb, hh] = out
            r_ref[qb, hh] = r

    for qb in range(qblocks):
        def more_keys_matter(state, qb=qb):
            n_done, r_max = state
            return jnp.logical_and(n_done < first[qb], r_max > PRUNE_LOG2)

        def earlier_chunk(state, qb=qb):
            n_done, _ = state
            start = pl.multiple_of((first[qb] - 1 - n_done) * tq, tq)
            for hh, cols in enumerate(head_cols):
                _sb_chunk(q_ref[rows[qb], cols], k_ref[pl.ds(start, tq), cols],
                          v_ref[pl.ds(start, tq), cols], None, tri,
                          r_ref.at[qb, hh], acc_ref.at[qb, hh], tq)
            return n_done + 1, jnp.max(r_ref[qb])

        _, r_max = lax.while_loop(more_keys_matter, earlier_chunk,
                                  (jnp.int32(0), jnp.max(r_ref[qb])))

        @pl.when(r_max > PRUNE_LOG2)
        def _(qb=qb):
            n_pad = km_ref.shape[0]
            is_meta = lax.broadcasted_iota(jnp.int32, (tq, n_pad), 1) < n_meta
            for hh, cols in enumerate(head_cols):
                _sb_chunk(q_ref[rows[qb], cols], km_ref[:, cols], vm_ref[:, cols], is_meta,
                          tri_ref[0:n_pad, 0:n_pad], r_ref.at[qb, hh], acc_ref.at[qb, hh],
                          n_pad)

        for hh, cols in enumerate(head_cols):
            o_ref[rows[qb], cols] = acc_ref[qb, hh].astype(o_ref.dtype)


def _sb_attention(qkv, meta_kv, *, n_batch, seq, n_heads, n_meta, tq, heads, qblocks):
    m = qkv.shape[0]
    d = n_heads * HEAD_DIM
    n_pad = meta_kv.shape[0]
    tstep = qblocks * tq
    nq = seq // tstep
    assert seq % tstep == 0 and seq >= 2 * tq and n_pad <= tq and n_heads % heads == 0
    groups = n_heads // heads
    width = heads * HEAD_DIM
    tri = jnp.where(lax.broadcasted_iota(jnp.int32, (tq, tq), 0)
                    >= lax.broadcasted_iota(jnp.int32, (tq, tq), 1), -1.0, 0.0).astype(BF16)
    pairs = qblocks * heads
    est = (2 * 2 * tstep * width * 2 + 2 * 2 * seq * width * 2
           + 2 * 2 * n_pad * width * 2 + 2 * tq * tq * 2
           + pairs * tq * (V7X_LANES + HEAD_DIM) * 4 + pairs * 4 * tq * 2 * tq * 4)
    kern = functools.partial(_sb_kernel, tq=tq, n_meta=n_meta, heads=heads, qblocks=qblocks)
    return pl.pallas_call(
        kern,
        out_shape=jax.ShapeDtypeStruct((m, d), BF16),
        grid=(n_batch, groups, nq),
        in_specs=[
            pl.BlockSpec((tstep, width), lambda b, g, i: (b * nq + i, g)),
            pl.BlockSpec((seq, width), lambda b, g, i: (b, groups + g)),
            pl.BlockSpec((seq, width), lambda b, g, i: (b, 2 * groups + g)),
            pl.BlockSpec((n_pad, width), lambda b, g, i: (0, groups + g)),
            pl.BlockSpec((n_pad, width), lambda b, g, i: (0, 2 * groups + g)),
            pl.BlockSpec((tq, tq), lambda b, g, i: (0, 0)),
        ],
        out_specs=pl.BlockSpec((tstep, width), lambda b, g, i: (b * nq + i, g)),
        scratch_shapes=[pltpu.VMEM((qblocks, heads, tq, V7X_LANES), F32),
                        pltpu.VMEM((qblocks, heads, tq, HEAD_DIM), F32)],
        compiler_params=pltpu.CompilerParams(
            dimension_semantics=("parallel", "parallel", "arbitrary"),
            vmem_limit_bytes=_vmem_limit(est)),
        name="stick_breaking_attention",
    )(qkv, qkv, qkv, meta_kv, meta_kv, tri)


def _oproj_kernel(x_ref, o_ref, w_ref, out_ref):
    out_ref[...] = x_ref[...] + jnp.dot(o_ref[...], w_ref[...], preferred_element_type=F32)


def _out_proj(x, o, w_o, *, tm, tn):
    m, d = x.shape
    est = 2 * (tm * tn * 4 * 2 + tm * d * 2 + d * tn * 2)
    return pl.pallas_call(
        _oproj_kernel,
        out_shape=jax.ShapeDtypeStruct((m, d), F32),
        grid=(m // tm, d // tn),
        in_specs=[
            pl.BlockSpec((tm, tn), lambda i, j: (i, j)),
            pl.BlockSpec((tm, d), lambda i, j: (i, 0)),
            pl.BlockSpec((d, tn), lambda i, j: (0, j)),
        ],
        out_specs=pl.BlockSpec((tm, tn), lambda i, j: (i, j)),
        compiler_params=pltpu.CompilerParams(
            dimension_semantics=("parallel", "arbitrary"),
            vmem_limit_bytes=_vmem_limit(est)),
        name="attn_out_proj",
    )(x, o, w_o)


def kernel(x, meta, ffn_norm, ffn_w_in, ffn_w_out, mix_norm, pool_w, pool_scale,
           sb_w_qkv, sb_qk_norm, sb_w_o):
    n_batch, seq, d = x.shape
    n_meta = meta.shape[0]
    d_ff = ffn_w_out.shape[2]
    n_heads = d // HEAD_DIM
    assert ffn_norm.shape[0] == 2 and pool_w.shape[0] == 1 and sb_w_qkv.shape[0] == 1
    assert n_meta == MAX_WINDOW
    assert (n_batch * seq) % FFN_ROW_TILE == 0 and seq % ROW_TILE == 0
    assert d_ff % FFN_FF_TILE == 0 and d % QKV_COL_TILE == 0 and d % OUT_PROJ_COL_TILE == 0

    w00 = (ffn_w_in[0, 0].astype(BF16), (0.5 * ffn_w_out[0, 0]).astype(BF16))
    w_pool = pool_w[0].astype(BF16)
    w_qkv = sb_w_qkv[0].astype(BF16)
    w_o = sb_w_o[0].astype(BF16)

    def ffn(h, layer, half, weights, tm, next_sublayer=None):
        nxt = None if next_sublayer is None else (ffn_w_in, ffn_w_out) + next_sublayer
        return _ffn_half(h, ffn_norm[layer, half], weights[0], weights[1], nxt,
                         tm=tm, tf=FFN_FF_TILE)

    h = x.reshape(n_batch * seq, d)
    h, *w01 = ffn(h, 0, 0, w00, FFN_ROW_TILE, next_sublayer=(0, 1))

    hm = ffn(meta.astype(F32), 0, 0, w00, n_meta)
    hm_pre_pool = hm
    hm = _pool_mixer(hm, jnp.zeros_like(hm), mix_norm[0], w_pool, pool_scale[0],
                     tm=n_meta, seq=n_meta, pos0=0)

    h = _pool_mixer(h, hm_pre_pool, mix_norm[0], w_pool, pool_scale[0],
                    tm=ROW_TILE, seq=seq, pos0=n_meta)
    h, *w10 = ffn(h, 0, 1, w01, FFN_ROW_TILE, next_sublayer=(1, 0))
    h, *w11 = ffn(h, 1, 0, w10, FFN_ROW_TILE, next_sublayer=(1, 1))

    hm = ffn(hm, 0, 1, w01, n_meta)
    hm = ffn(hm, 1, 0, w10, n_meta)
    meta_qkv = _qkv_proj(hm, mix_norm[1], w_qkv, sb_qk_norm[0],
                         tm=n_meta, tn=META_QKV_COL_TILE)
    meta_kv = jnp.pad(meta_qkv, ((0, V7X_LANES - n_meta), (0, 0)))

    qkv = _qkv_proj(h, mix_norm[1], w_qkv, sb_qk_norm[0], tm=ROW_TILE, tn=QKV_COL_TILE)
    o = _sb_attention(qkv, meta_kv, n_batch=n_batch, seq=seq, n_heads=n_heads,
                      n_meta=n_meta, tq=ATTN_Q_ROWS, heads=ATTN_HEADS_PER_STEP,
                      qblocks=ATTN_QBLOCKS_PER_STEP)
    h = _out_proj(h, o, w_o, tm=ROW_TILE, tn=OUT_PROJ_COL_TILE)
    h = ffn(h, 1, 1, w11, FFN_ROW_TILE)
    return h.reshape(n_batch, seq, d)
```
